```python
import math
import jax, jax.numpy as jnp
from jax import lax
import numpy as np

D_MODEL = 1024
BATCH = 8
SEQ = 2048
DEPTH = 1
DEC_BATCH = 128
DEC_SEQ = 1
PAST_LEN = 2048
PAGE_SIZE = 128

DA_HEADS = 4
DA_HD = 64
DA_WIDTH = DA_HEADS * 2 * DA_HD
SA_HEADS = 8
SA_HD = 64
SA_WIDTH = SA_HEADS * SA_HD
IDX_HEADS = 8
IDX_HD = 64
TOPK_MAX = 256
PEER_HEADS = 8
PEER_NKEYS = 128
PEER_N = PEER_NKEYS * PEER_NKEYS
PEER_QD = 256
PEER_TOPK = 16
PEER_BLOCK = 128
Q_BLOCK = 128
LN_EPS = 1e-5
RMS_EPS = 1e-5
ALPHA = (2.0 * DEPTH) ** 0.25
BETA = (8.0 * DEPTH) ** -0.25
_SPLITS = (DA_WIDTH, DA_WIDTH, DA_WIDTH, SA_WIDTH, SA_WIDTH, SA_WIDTH,
           IDX_HEADS * IDX_HD, IDX_HD, IDX_HEADS, D_MODEL, D_MODEL)
N_IN = sum(_SPLITS)

kernel_name = 'hybrid_diffattn_dsa_peer_step'


def _alibi_slopes(n):
    return jnp.asarray([2.0 ** (-8.0 * (i + 1) / n) for i in range(n)], dtype=jnp.float32)


def _layer_norm(x, g, b):
    xf = x.astype(jnp.float32)
    mu = jnp.mean(xf, -1, keepdims=True)
    var = jnp.mean(jnp.square(xf - mu), -1, keepdims=True)
    return ((xf - mu) * lax.rsqrt(var + LN_EPS) * g + b).astype(x.dtype)


def _in_proj(x, w_in):
    B, S, _ = x.shape
    z = jnp.einsum('bsd,dn->bsn', x, w_in)
    offs = np.cumsum(_SPLITS)[:-1].tolist()
    qa, ka, va, qs, ks, vs, qi, ki, wi, ga, gb = jnp.split(z, offs, axis=-1)
    return (qa.reshape(B, S, DA_HEADS, 2, DA_HD), ka.reshape(B, S, DA_HEADS, 2, DA_HD),
            va.reshape(B, S, DA_HEADS, 2 * DA_HD), qs.reshape(B, S, SA_HEADS, SA_HD),
            ks.reshape(B, S, SA_HEADS, SA_HD), vs.reshape(B, S, SA_HEADS, SA_HD),
            qi.reshape(B, S, IDX_HEADS, IDX_HD), ki, wi, ga, gb)


def _diff_lambda(lq1, lk1, lq2, lk2, lam_init):
    f = lambda a, b: jnp.exp(jnp.sum(a.astype(jnp.float32) * b.astype(jnp.float32)))
    return f(lq1, lk1) - f(lq2, lk2) + lam_init


def _diff_attend(q, k, v, q_pos, k_pos, lam, lam_init, subln_g):
    dist = (q_pos[:, None] - k_pos[None, :]).astype(jnp.float32)
    bias = jnp.where(dist >= 0, -_alibi_slopes(DA_HEADS)[:, None, None] * dist, -jnp.inf)
    s = jnp.einsum('bqhid,bkhid->bhiqk', q, k).astype(jnp.float32) * (DA_HD ** -0.5) + bias[None, :, None]
    p = jax.nn.softmax(s, axis=-1)
    attn = p[:, :, 0] - lam * p[:, :, 1]
    o = jnp.einsum('bhqk,bkhe->bqhe', attn.astype(v.dtype), v).astype(jnp.float32)
    o = o * lax.rsqrt(jnp.mean(o * o, -1, keepdims=True) + RMS_EPS) * subln_g * (1.0 - lam_init)
    return o.astype(v.dtype)


def _index_scores(qi, wi, ki):
    r = jax.nn.relu(jnp.einsum('bqhd,bld->bqhl', qi, ki).astype(jnp.float32) * (IDX_HD ** -0.5))
    return jnp.einsum('bqh,bqhl->bql', wi.astype(jnp.float32) * (IDX_HEADS ** -0.5), r)


def _select(scores, q_pos, k_pos, n_sel):
    valid = k_pos[None, :] <= q_pos[:, None]
    _, idx = lax.top_k(jnp.where(valid[None], scores, -jnp.inf), n_sel)
    return idx


def _sparse_attend(q, k_sel, v_sel, q_pos, sel_pos):
    dist = (q_pos[None, :, None] - sel_pos).astype(jnp.float32)[:, None]
    bias = jnp.where(dist >= 0, -_alibi_slopes(SA_HEADS)[None, :, None, None] * dist, -jnp.inf)
    s = jnp.einsum('bqhd,bqkhd->bhqk', q, k_sel).astype(jnp.float32) * (SA_HD ** -0.5) + bias
    p = jax.nn.softmax(s, axis=-1)
    return jnp.einsum('bhqk,bqkhd->bqhd', p.astype(v_sel.dtype), v_sel)


def _prompt_mixer(x, w_in, lam, lam_init, subln_g):
    B, S, _ = x.shape
    qa, ka, va, qs, ks, vs, qi, ki, wi, ga, gb = _in_proj(x, w_in)
    pos = jnp.arange(S)
    n_sel = min(TOPK_MAX, S // 4)
    bidx = jnp.arange(B)[:, None, None]

    def block(i):
        start = i * Q_BLOCK
        sl = lambda t: lax.dynamic_slice_in_dim(t, start, Q_BLOCK, axis=1)
        q_pos = start + jnp.arange(Q_BLOCK)
        a = _diff_attend(sl(qa), ka, va, q_pos, pos, lam, lam_init, subln_g)
        idx = _select(_index_scores(sl(qi), sl(wi), ki), q_pos, pos, n_sel)
        o = _sparse_attend(sl(qs), ks[bidx, idx], vs[bidx, idx], q_pos, idx)
        return a, o

    a, o = lax.map(block, jnp.arange(S // Q_BLOCK))
    a = jnp.moveaxis(a, 0, 1).reshape(B, S, DA_WIDTH)
    o = jnp.moveaxis(o, 0, 1).reshape(B, S, SA_WIDTH)
    return a, o, ga, gb, (ka, va, ks, vs, ki)


def _sample_mixer(x, w_in, lam, lam_init, subln_g, c_da_k, c_da_v, c_sa_k, c_sa_v, c_idx_k, page_table):
    B, S, _ = x.shape
    past = page_table.shape[1] * PAGE_SIZE
    L = past + S
    qa, ka, va, qs, ks, vs, qi, ki, wi, ga, gb = _in_proj(x, w_in)
    q_pos = past + jnp.arange(S)
    k_pos = jnp.arange(L)

    def paged(c):
        return c[page_table].reshape((B, past) + c.shape[2:])

    ka_all = jnp.concatenate([paged(c_da_k), ka], axis=1)
    va_all = jnp.concatenate([paged(c_da_v), va], axis=1)
    a = _diff_attend(qa, ka_all, va_all, q_pos, k_pos, lam, lam_init, subln_g)
    ki_all = jnp.concatenate([paged(c_idx_k), ki], axis=1)
    n_sel = min(TOPK_MAX, L // 4)
    idx = _select(_index_scores(qi, wi, ki_all), q_pos, k_pos, n_sel)
    bidx = jnp.arange(B)[:, None, None]
    in_past = (idx < past)[..., None, None]
    pidx = jnp.minimum(idx, past - 1)
    phys = page_table[bidx, pidx // PAGE_SIZE]
    row = pidx % PAGE_SIZE
    nidx = jnp.clip(idx - past, 0, S - 1)
    pick = lambda c, new: jnp.where(in_past, c[phys, row], new[bidx, nidx])
    o = _sparse_attend(qs, pick(c_sa_k, ks), pick(c_sa_v, vs), q_pos, idx)
    return a.reshape(B, S, DA_WIDTH), o.reshape(B, S, SA_WIDTH), ga, gb, (ka, va, ks, vs, ki)


def _peer(x, wq, sub_keys, u_tab, v_tab):
    shp = x.shape
    t = x.reshape(-1, D_MODEL)
    T = t.shape[0]
    q = (t @ wq).reshape(T, PEER_HEADS, 2, PEER_QD // 2)
    s = jnp.einsum('thcd,hcnd->thcn', q, sub_keys).astype(jnp.float32)
    sv, si = lax.top_k(s, PEER_TOPK)
    cand = sv[..., 0, :, None] + sv[..., 1, None, :]
    cand_id = si[..., 0, :, None] * PEER_NKEYS + si[..., 1, None, :]
    top_s, top_c = lax.top_k(cand.reshape(T, PEER_HEADS, -1), PEER_TOPK)
    experts = jnp.take_along_axis(cand_id.reshape(T, PEER_HEADS, -1), top_c, axis=-1)
    g = jax.nn.softmax(top_s, axis=-1)
    pad = (-T) % PEER_BLOCK
    nb = (T + pad) // PEER_BLOCK
    tp = jnp.pad(t, ((0, pad), (0, 0))).reshape(nb, PEER_BLOCK, D_MODEL)
    ep = jnp.pad(experts, ((0, pad), (0, 0), (0, 0))).reshape(nb, PEER_BLOCK, PEER_HEADS, PEER_TOPK)
    gp = jnp.pad(g, ((0, pad), (0, 0), (0, 0))).reshape(nb, PEER_BLOCK, PEER_HEADS, PEER_TOPK)

    def blk(args):
        xb, eb, gw = args
        pre = jnp.einsum('cd,chkd->chk', xb, u_tab[eb]).astype(jnp.float32)
        act = jax.nn.gelu(pre, approximate=False) * gw
        return jnp.einsum('chk,chkd->cd', act.astype(xb.dtype), v_tab[eb])

    out = lax.map(blk, (tp, ep, gp))
    return out.reshape(-1, D_MODEL)[:T].reshape(shp)


def _layer_tail(x, a, o, ga, gb, w_a_up, w_b_up, w_out, ln1_g, ln1_b, peer_wq, peer_sub_keys, peer_u, peer_v, ln2_g, ln2_b):
    merged = jax.nn.sigmoid(ga) * (a @ w_a_up) + jax.nn.sigmoid(gb) * (o @ w_b_up)
    h = _layer_norm(ALPHA * x + merged @ w_out, ln1_g, ln1_b)
    return _layer_norm(ALPHA * h + _peer(h, peer_wq, peer_sub_keys, peer_u, peer_v), ln2_g, ln2_b)


def setup_inputs(seed: int = 0) -> dict:
    key = jax.random.key(seed)
    ks = jax.random.split(key, 24)
    f32 = jnp.float32
    nrm = lambda k, shp, s: jax.random.normal(k, shp, f32) * s
    n_pages = PAST_LEN // PAGE_SIZE
    n_used = DEC_BATCH * n_pages
    n_pool = n_used + max(1, n_used // 4)
    page_table = jax.random.permutation(ks[7], n_pool)[:n_used].reshape(DEC_BATCH, n_pages).astype(jnp.int32)
    col_scale = [1.0, 1.0, BETA, 1.0, 1.0, BETA, 1.0, 1.0, 1.0, 1.0, 1.0]
    col_scale = jnp.concatenate([jnp.full((n,), s, f32) for n, s in zip(_SPLITS, col_scale)])
    return {
        'x_prompt': nrm(ks[0], (BATCH, SEQ, D_MODEL), 1.0),
        'x_sample': nrm(ks[1], (DEC_BATCH, DEC_SEQ, D_MODEL), 1.0),
        'cache_da_k': nrm(ks[2], (DEPTH, n_pool, PAGE_SIZE, DA_HEADS, 2, DA_HD), 1.0),
        'cache_da_v': nrm(ks[3], (DEPTH, n_pool, PAGE_SIZE, DA_HEADS, 2 * DA_HD), BETA),
        'cache_sa_k': nrm(ks[4], (DEPTH, n_pool, PAGE_SIZE, SA_HEADS, SA_HD), 1.0),
        'cache_sa_v': nrm(ks[5], (DEPTH, n_pool, PAGE_SIZE, SA_HEADS, SA_HD), BETA),
        'cache_idx_k': nrm(ks[6], (DEPTH, n_pool, PAGE_SIZE, IDX_HD), 1.0),
        'page_table': page_table,
        'w_in': nrm(ks[8], (DEPTH, D_MODEL, N_IN), D_MODEL ** -0.5) * col_scale,
        'lambda_q1': nrm(ks[9], (DEPTH, DA_HD), 0.1),
        'lambda_k1': nrm(ks[10], (DEPTH, DA_HD), 0.1),
        'lambda_q2': nrm(ks[11], (DEPTH, DA_HD), 0.1),
        'lambda_k2': nrm(ks[12], (DEPTH, DA_HD), 0.1),
        'da_subln_g': 1.0 + nrm(ks[13], (DEPTH, 2 * DA_HD), 0.02),
        'w_a_up': nrm(ks[14], (DEPTH, DA_WIDTH, D_MODEL), DA_WIDTH ** -0.5),
        'w_b_up': nrm(ks[15], (DEPTH, SA_WIDTH, D_MODEL), SA_WIDTH ** -0.5),
        'w_out': nrm(ks[16], (DEPTH, D_MODEL, D_MODEL), BETA * D_MODEL ** -0.5),
        'ln1_g': 1.0 + nrm(ks[17], (DEPTH, D_MODEL), 0.02),
        'ln1_b': nrm(ks[18], (DEPTH, D_MODEL), 0.02),
        'peer_wq': nrm(ks[19], (DEPTH, D_MODEL, PEER_HEADS * PEER_QD), D_MODEL ** -0.5),
        'peer_sub_keys': nrm(ks[20], (DEPTH, PEER_HEADS, 2, PEER_NKEYS, PEER_QD // 2), (PEER_QD // 2) ** -0.5),
        'peer_u': nrm(ks[21], (DEPTH, PEER_N, D_MODEL), D_MODEL ** -0.5),
        'peer_v': nrm(ks[22], (DEPTH, PEER_N, D_MODEL), BETA),
        'ln2_g': 1.0 + nrm(ks[23], (DEPTH, D_MODEL), 0.02),
        'ln2_b': nrm(jax.random.fold_in(ks[23], 1), (DEPTH, D_MODEL), 0.02),
    }


def reference(x_prompt, x_sample, cache_da_k, cache_da_v, cache_sa_k, cache_sa_v, cache_idx_k, page_table,
              w_in, lambda_q1, lambda_k1, lambda_q2, lambda_k2, da_subln_g, w_a_up, w_b_up, w_out,
              ln1_g, ln1_b, peer_wq, peer_sub_keys, peer_u, peer_v, ln2_g, ln2_b):
    y_p, y_s = x_prompt, x_sample
    p_new, s_new = [], []
    for l in range(DEPTH):
        lam_init = 0.8 - 0.6 * math.exp(-0.3 * l)
        lam = _diff_lambda(lambda_q1[l], lambda_k1[l], lambda_q2[l], lambda_k2[l], lam_init)
        tail = (w_a_up[l], w_b_up[l], w_out[l], ln1_g[l], ln1_b[l], peer_wq[l], peer_sub_keys[l],
                peer_u[l], peer_v[l], ln2_g[l], ln2_b[l])
        a, o, ga, gb, st = _prompt_mixer(y_p, w_in[l], lam, lam_init, da_subln_g[l])
        y_p = _layer_tail(y_p, a, o, ga, gb, *tail)
        p_new.append(st)
        a, o, ga, gb, st = _sample_mixer(y_s, w_in[l], lam, lam_init, da_subln_g[l], cache_da_k[l], cache_da_v[l],
                                         cache_sa_k[l], cache_sa_v[l], cache_idx_k[l], page_table)
        y_s = _layer_tail(y_s, a, o, ga, gb, *tail)
        s_new.append(st)
    stk = lambda lst, j: jnp.stack([e[j] for e in lst])
    return (y_p, y_s,
            stk(p_new, 0), stk(p_new, 1), stk(p_new, 2), stk(p_new, 3), stk(p_new, 4),
            stk(s_new, 0), stk(s_new, 1), stk(s_new, 2), stk(s_new, 3), stk(s_new, 4))
```

```python
import functools
import math

import jax
import jax.numpy as jnp
from jax import lax
from jax.experimental import pallas as pl
from jax.experimental.pallas import tpu as pltpu

F32 = jnp.float32
BF16 = jnp.bfloat16
NEG_INF = float("-inf")

LANES = 128
DA_HEADS = 4
DA_HD = 64
DA_WIDTH = DA_HEADS * 2 * DA_HD
SA_HEADS = 8
SA_HD = 64
SA_WIDTH = SA_HEADS * SA_HD
IDX_HEADS = 8
IDX_HD = 64
TOPK_MAX = 256
PEER_HEADS = 8
PEER_NKEYS = 128
PEER_QD = 256
PEER_TOPK = 16
LN_EPS = 1e-5
RMS_EPS = 1e-5
VMEM_LIMIT = 56 * 1024 * 1024


def _alibi_slopes(n):
    return [2.0 ** (-8.0 * (i + 1) / n) for i in range(n)]


def _dot(a, b):
    return jnp.dot(a, b, preferred_element_type=F32)


def _dot_nt(a, b):
    return lax.dot_general(a, b, (((1,), (1,)), ((), ())), preferred_element_type=F32)


def _fold_lanes(x, op):
    parts = [x[:, t * LANES:(t + 1) * LANES] for t in range(x.shape[1] // LANES)]
    return functools.reduce(op, parts)


def _params(sem):
    return pltpu.CompilerParams(dimension_semantics=sem, vmem_limit_bytes=VMEM_LIMIT)


def _layer_norm(x, g, b):
    mu = jnp.mean(x, axis=-1, keepdims=True)
    xc = x - mu
    var = jnp.mean(xc * xc, axis=-1, keepdims=True)
    return xc * lax.rsqrt(var + LN_EPS) * g + b


_SEG = dict(qa=0, ka=512, va=1024, qs=1536, ks=2048, vs=2560, qi=3072, ki=3584, wi=3712, ga=3840)
_W_COLS = 3840


def _arrange_w_in(w_in, d_model):
    offs = [0]
    for n in (DA_WIDTH, DA_WIDTH, DA_WIDTH, SA_WIDTH, SA_WIDTH, SA_WIDTH,
              IDX_HEADS * IDX_HD, IDX_HD, IDX_HEADS, d_model, d_model):
        offs.append(offs[-1] + n)
    ki = w_in[:, offs[7]:offs[8]]
    wi = w_in[:, offs[8]:offs[9]]
    pad = jnp.zeros((w_in.shape[0], LANES - IDX_HEADS), w_in.dtype)
    return jnp.concatenate([w_in[:, :offs[7]], ki, ki, wi, pad, w_in[:, offs[9]:]], axis=1).astype(BF16)


def _in_proj_kernel(x_ref, w_ref, qa_o, kaf_o, kab_o, vaf_o, vab_o, qs_o, ksf_o, ksb_o, vsf_o, vsb_o,
                    qi_o, kif_o, ki2_o, wi_o, ga_o, gb_o, *, d_model):
    xb = x_ref[...].astype(BF16)

    def seg(name, n):
        a = _SEG[name]
        return _dot(xb, w_ref[:, a:a + n])

    qa_o[...] = (seg("qa", DA_WIDTH) * (DA_HD ** -0.5)).astype(BF16)
    z = seg("ka", DA_WIDTH)
    kaf_o[...] = z
    kab_o[...] = z.astype(BF16)
    z = seg("va", DA_WIDTH)
    vaf_o[...] = z
    vab_o[...] = z.astype(BF16)
    qs_o[...] = (seg("qs", SA_WIDTH) * (SA_HD ** -0.5)).astype(BF16)
    z = seg("ks", SA_WIDTH)
    ksf_o[...] = z
    ksb_o[...] = z.astype(BF16)
    z = seg("vs", SA_WIDTH)
    vsf_o[...] = z
    vsb_o[...] = z.astype(BF16)
    qi_o[...] = (seg("qi", IDX_HEADS * IDX_HD) * (IDX_HD ** -0.5)).astype(BF16)
    z = seg("ki", 2 * IDX_HD)
    kif_o[...] = z[:, :IDX_HD]
    ki2_o[...] = z.astype(BF16)
    wi_o[...] = seg("wi", LANES) * (IDX_HEADS ** -0.5)
    ga_o[...] = seg("ga", d_model)
    gb_o[...] = _dot(xb, w_ref[:, _SEG["ga"] + d_model:_SEG["ga"] + 2 * d_model])


def _in_proj(x, w_arr):
    t, d_model = x.shape
    tm = min(t, 256)
    assert t % tm == 0
    row = lambda n: pl.BlockSpec((tm, n), lambda i: (i, 0))
    widths = [(512, BF16), (512, F32), (512, BF16), (512, F32), (512, BF16), (512, BF16), (512, F32),
              (512, BF16), (512, F32), (512, BF16), (512, BF16), (IDX_HD, F32), (LANES, BF16), (LANES, F32),
              (d_model, F32), (d_model, F32)]
    return pl.pallas_call(
        functools.partial(_in_proj_kernel, d_model=d_model),
        out_shape=[jax.ShapeDtypeStruct((t, n), dt) for n, dt in widths],
        grid=(t // tm,),
        in_specs=[row(d_model), pl.BlockSpec(w_arr.shape, lambda i: (0, 0))],
        out_specs=[row(n) for n, _ in widths],
        compiler_params=_params(("parallel",)),
        name="in_proj",
    )(x, w_arr)


def _sortable_key(x):
    bits = lax.bitcast_convert_type(x + 0.0, jnp.int32)
    return bits ^ ((bits >> 31) & jnp.int32(0x7FFFFFFF))


def _topk_mask(key_scr, msk_scr, nk, k_sel, cw, idx_bits):
    rows = key_scr.shape[1]
    pos = lax.broadcasted_iota(jnp.int32, (rows, cw), 1)

    def count(pred_fn):
        def body(c, acc):
            hit = jnp.where(pred_fn(key_scr[c], pos + c * cw), 1.0, 0.0)
            return acc + _fold_lanes(hit, jnp.add)
        acc = lax.fori_loop(0, nk, body, jnp.zeros((rows, LANES), F32))
        return jnp.sum(acc, axis=1, keepdims=True)

    def thr_bit(it, thr):
        cand = thr + lax.shift_left(jnp.int32(1), 31 - it)
        cnt = count(lambda key, p: key >= cand)
        return jnp.where(cnt >= k_sel, cand, thr)

    thr = lax.fori_loop(0, 32, thr_bit, jnp.full((rows, 1), jnp.iinfo(jnp.int32).min, jnp.int32))
    need = k_sel - count(lambda key, p: key > thr)

    def lim_bit(it, lim):
        cand = lim + lax.shift_left(jnp.int32(1), idx_bits - 1 - it)
        cnt = count(lambda key, p: (key == thr) & (p < cand))
        return jnp.where(cnt < need, cand, lim)

    lim = lax.fori_loop(0, idx_bits, lim_bit, jnp.zeros((rows, 1), jnp.int32))

    def write(c, carry):
        key = key_scr[c]
        sel = (key > thr) | ((key == thr) & (pos + c * cw <= lim))
        msk_scr[c] = jnp.where(sel, 0.0, NEG_INF)
        return carry

    lax.fori_loop(0, nk, write, 0)


def _attend(units, q_of, kcols, vcols, slopes, k_ref, v_ref, msk_scr, s_scr, m_scr, l_scr, acc_scr,
            q0, nk, sq, ck):
    rel = (lax.broadcasted_iota(jnp.int32, (sq, ck), 0)
           - lax.broadcasted_iota(jnp.int32, (sq, ck), 1)).astype(F32)
    m_scr[...] = jnp.full(m_scr.shape, NEG_INF, F32)
    l_scr[...] = jnp.zeros(l_scr.shape, F32)
    acc_scr[...] = jnp.zeros(acc_scr.shape, F32)

    def pass1(kc, carry):
        k0 = pl.multiple_of(kc * ck, ck)
        dist = (q0 - k0).astype(F32) + rel
        ok = dist >= 0.0
        extra = None if msk_scr is None else msk_scr[kc]
        for u in units:
            kb = k_ref[pl.ds(k0, ck), kcols(u)]
            s = _dot_nt(q_of(u), kb) - slopes[u] * dist
            if extra is not None:
                s = s + extra
            s = jnp.where(ok, s, NEG_INF)
            s_scr[u, kc] = s
            m_scr[u] = jnp.maximum(m_scr[u], _fold_lanes(s, jnp.maximum))
        return carry

    lax.fori_loop(0, nk, pass1, 0)
    m_col = [jnp.max(m_scr[u], axis=1, keepdims=True) for u in units]

    def pass2(kc, carry):
        k0 = pl.multiple_of(kc * ck, ck)
        for u in units:
            p = jnp.exp(s_scr[u, kc] - m_col[u])
            l_scr[u] = l_scr[u] + _fold_lanes(p, jnp.add)
            acc_scr[u] = acc_scr[u] + _dot(p.astype(BF16), v_ref[pl.ds(k0, ck), vcols(u)])
        return carry

    lax.fori_loop(0, nk, pass2, 0)


def _half_masks(shape):
    lane = lax.broadcasted_iota(jnp.int32, shape, 1)
    return lane < (LANES // 2), lane >= (LANES // 2)


def _diff_lambda(lv, lam_init):
    d1 = jnp.sum(lv[0:1, :] * lv[1:2, :], axis=1, keepdims=True)
    d2 = jnp.sum(lv[2:3, :] * lv[3:4, :], axis=1, keepdims=True)
    return jnp.exp(d1) - jnp.exp(d2) + lam_init


def _diff_combine(o0, o1, lam, g, lam_init):
    o = o0 - lam * o1
    o = o * lax.rsqrt(jnp.mean(o * o, axis=-1, keepdims=True) + RMS_EPS)
    return o * g * (1.0 - lam_init)


def _diff_attn_kernel(lv_ref, g_ref, q_ref, k_ref, v_ref, o_ref, s_scr, m_scr, l_scr, acc_scr,
                      *, sq, ck, lam_init):
    j = pl.program_id(1)
    q0 = j * sq
    nk = (q0 + sq - 1) // ck + 1
    lo, hi = _half_masks((sq, LANES))
    qm = []
    for h in range(DA_HEADS):
        qh = q_ref[:, h * LANES:(h + 1) * LANES]
        qm.append(jnp.where(lo, qh, jnp.zeros_like(qh)))
        qm.append(jnp.where(hi, qh, jnp.zeros_like(qh)))
    units = list(range(2 * DA_HEADS))
    cols = lambda u: slice((u // 2) * LANES, (u // 2 + 1) * LANES)
    slopes = [s for s in _alibi_slopes(DA_HEADS) for _ in range(2)]
    _attend(units, lambda u: qm[u], cols, cols, slopes, k_ref, v_ref, None, s_scr, m_scr, l_scr, acc_scr,
            q0, nk, sq, ck)
    lam = _diff_lambda(lv_ref[...], lam_init)
    for h in range(DA_HEADS):
        o0 = acc_scr[2 * h] / jnp.sum(l_scr[2 * h], axis=1, keepdims=True)
        o1 = acc_scr[2 * h + 1] / jnp.sum(l_scr[2 * h + 1], axis=1, keepdims=True)
        o = _diff_combine(o0, o1, lam, g_ref[...], lam_init)
        o_ref[:, h * LANES:(h + 1) * LANES] = o.astype(o_ref.dtype)


def _prompt_diff_attn(lv, g, qa, ka, va, lam_init):
    b, s, _ = qa.shape
    sq = min(s, 128)
    ck = min(s, 512)
    nc = s // ck
    u = 2 * DA_HEADS
    blk_q = pl.BlockSpec((None, sq, DA_WIDTH), lambda bi, j: (bi, j, 0))
    blk_kv = pl.BlockSpec((None, s, DA_WIDTH), lambda bi, j: (bi, 0, 0))
    full = lambda a: pl.BlockSpec(a.shape, lambda bi, j: (0,) * a.ndim)
    return pl.pallas_call(
        functools.partial(_diff_attn_kernel, sq=sq, ck=ck, lam_init=lam_init),
        out_shape=jax.ShapeDtypeStruct((b, s, DA_WIDTH), BF16),
        grid=(b, s // sq),
        in_specs=[full(lv), full(g), blk_q, blk_kv, blk_kv],
        out_specs=blk_q,
        scratch_shapes=[pltpu.VMEM((u, nc, sq, ck), F32), pltpu.VMEM((u, sq, LANES), F32),
                        pltpu.VMEM((u, sq, LANES), F32), pltpu.VMEM((u, sq, LANES), F32)],
        compiler_params=_params(("parallel", "arbitrary")),
        name="prompt_diff_attn",
    )(lv, g, qa, ka, va)


def _index_scores_chunk(qim, wi, kblk):
    total = None
    for h in range(IDX_HEADS):
        r = jnp.maximum(_dot_nt(qim[h], kblk), 0.0)
        term = wi[:, h:h + 1] * r
        total = term if total is None else total + term
    return total


def _sparse_attn_kernel(qi_ref, wi_ref, ki_ref, qs_ref, ks_ref, vs_ref, o_ref,
                        key_scr, msk_scr, s_scr, m_scr, l_scr, acc_scr, *, sq, ck, k_sel, idx_bits):
    j = pl.program_id(1)
    q0 = j * sq
    nk = (q0 + sq - 1) // ck + 1
    lo, hi = _half_masks((sq, LANES))

    def masked(ref):
        out = []
        for p in range(ref.shape[1] // LANES):
            blk = ref[:, p * LANES:(p + 1) * LANES]
            out.append(jnp.where(lo, blk, jnp.zeros_like(blk)))
            out.append(jnp.where(hi, blk, jnp.zeros_like(blk)))
        return out

    qim = masked(qi_ref)
    wi = wi_ref[...]
    rel = (lax.broadcasted_iota(jnp.int32, (sq, ck), 0) - lax.broadcasted_iota(jnp.int32, (sq, ck), 1))

    def scores(kc, carry):
        k0 = pl.multiple_of(kc * ck, ck)
        tot = _index_scores_chunk(qim, wi, ki_ref[pl.ds(k0, ck), :])
        valid = (q0 - k0) + rel >= 0
        key_scr[kc] = _sortable_key(jnp.where(valid, tot, NEG_INF))
        return carry

    lax.fori_loop(0, nk, scores, 0)
    _topk_mask(key_scr, msk_scr, nk, k_sel, ck, idx_bits)

    qsm = masked(qs_ref)
    units = list(range(SA_HEADS))
    cols = lambda u: slice((u // 2) * LANES, (u // 2 + 1) * LANES)
    _attend(units, lambda u: qsm[u], cols, cols, _alibi_slopes(SA_HEADS), ks_ref, vs_ref, msk_scr,
            s_scr, m_scr, l_scr, acc_scr, q0, nk, sq, ck)
    for p in range(SA_HEADS // 2):
        oa = acc_scr[2 * p] / jnp.sum(l_scr[2 * p], axis=1, keepdims=True)
        ob = acc_scr[2 * p + 1] / jnp.sum(l_scr[2 * p + 1], axis=1, keepdims=True)
        o_ref[:, p * LANES:(p + 1) * LANES] = jnp.where(lo, oa, ob).astype(o_ref.dtype)


def _prompt_sparse_attn(qi, wi, ki2, qs, ks, vs):
    b, s, _ = qs.shape
    sq = min(s, 128)
    ck = min(s, 512)
    nc = s // ck
    k_sel = min(TOPK_MAX, s // 4)
    idx_bits = max(1, (s - 1).bit_length())
    blk_q = lambda w: pl.BlockSpec((None, sq, w), lambda bi, j: (bi, j, 0))
    blk_kv = lambda w: pl.BlockSpec((None, s, w), lambda bi, j: (bi, 0, 0))
    u = SA_HEADS
    return pl.pallas_call(
        functools.partial(_sparse_attn_kernel, sq=sq, ck=ck, k_sel=k_sel, idx_bits=idx_bits),
        out_shape=jax.ShapeDtypeStruct((b, s, SA_WIDTH), BF16),
        grid=(b, s // sq),
        in_specs=[blk_q(IDX_HEADS * IDX_HD), blk_q(LANES), blk_kv(LANES),
                  blk_q(SA_WIDTH), blk_kv(SA_WIDTH), blk_kv(SA_WIDTH)],
        out_specs=blk_q(SA_WIDTH),
        scratch_shapes=[pltpu.VMEM((nc, sq, ck), jnp.int32), pltpu.VMEM((nc, sq, ck), F32),
                        pltpu.VMEM((u, nc, sq, ck), F32), pltpu.VMEM((u, sq, LANES), F32),
                        pltpu.VMEM((u, sq, LANES), F32), pltpu.VMEM((u, sq, LANES), F32)],
        compiler_params=_params(("parallel", "arbitrary")),
        name="prompt_sparse_attn",
    )(qi, wi, ki2, qs, ks, vs)


def _sample_index_kernel(pt_ref, qi_ref, wi_ref, kin_ref, kpage_ref, o_ref, *, n_pages, page):
    p = pl.program_id(1)
    q = qi_ref[...]
    w = wi_ref[...]

    @pl.when(p < n_pages)
    def _():
        r = jnp.maximum(_dot_nt(q, kpage_ref[...].astype(BF16)), 0.0)
        o_ref[...] = jnp.sum(w * r, axis=0, keepdims=True)

    @pl.when(p == n_pages)
    def _():
        kn = kin_ref[...].astype(BF16).astype(F32)
        r = jnp.maximum(jnp.sum(q.astype(F32) * kn, axis=1, keepdims=True), 0.0)
        tot = jnp.sum(w * r, axis=0, keepdims=True)
        lane = lax.broadcasted_iota(jnp.int32, (1, page), 1)
        o_ref[...] = jnp.where(lane == 0, tot, NEG_INF)


def _sample_index_scores(page_table, qi8, wi8, ki_new, cache_idx_k, layer):
    bd, n_pages = page_table.shape
    page = cache_idx_k.shape[2]
    grid_spec = pltpu.PrefetchScalarGridSpec(
        num_scalar_prefetch=1,
        grid=(bd, n_pages + 1),
        in_specs=[
            pl.BlockSpec((None, IDX_HEADS, IDX_HD), lambda b, p, pt: (b, 0, 0)),
            pl.BlockSpec((None, IDX_HEADS, 1), lambda b, p, pt: (b, 0, 0)),
            pl.BlockSpec((None, 1, IDX_HD), lambda b, p, pt: (b, 0, 0)),
            pl.BlockSpec((None, None, page, IDX_HD),
                         lambda b, p, pt: (layer, pt[b, jnp.minimum(p, n_pages - 1)], 0, 0)),
        ],
        out_specs=pl.BlockSpec((None, None, 1, page), lambda b, p, pt: (b, p, 0, 0)),
    )
    return pl.pallas_call(
        functools.partial(_sample_index_kernel, n_pages=n_pages, page=page),
        out_shape=jax.ShapeDtypeStruct((bd, n_pages + 1, 1, page), F32),
        grid_spec=grid_spec,
        compiler_params=_params(("parallel", "arbitrary")),
        name="sample_index_scores",
    )(page_table, qi8, wi8, ki_new, cache_idx_k)


def _sample_select_kernel(sc_ref, msk_ref, key_scr, *, k_sel, idx_bits):
    nc = sc_ref.shape[1]
    for c in range(nc):
        key_scr[c] = _sortable_key(sc_ref[:, c, :])
    _topk_mask(key_scr, msk_ref, nc, k_sel, sc_ref.shape[2], idx_bits)


def _sample_select(scores, k_sel):
    bd, nc, page = scores.shape
    idx_bits = max(1, (nc * page - 1).bit_length())
    return pl.pallas_call(
        functools.partial(_sample_select_kernel, k_sel=k_sel, idx_bits=idx_bits),
        out_shape=jax.ShapeDtypeStruct((nc, bd, page), F32),
        grid=(1,),
        in_specs=[pl.BlockSpec(scores.shape, lambda i: (0, 0, 0))],
        out_specs=pl.BlockSpec((nc, bd, page), lambda i: (0, 0, 0)),
        scratch_shapes=[pltpu.VMEM((nc, bd, page), jnp.int32)],
        compiler_params=_params(("arbitrary",)),
        name="sample_select",
    )(scores)


def _online_update(s, v, m_scr, l_scr, acc_scr):
    m_old = m_scr[...]
    m_new = jnp.maximum(m_old, jnp.max(s, axis=1, keepdims=True))
    m_safe = jnp.where(m_new == NEG_INF, 0.0, m_new)
    alpha = jnp.exp(m_old - m_safe)
    p = jnp.exp(s - m_safe)
    l_scr[...] = alpha * l_scr[...] + jnp.sum(p, axis=1, keepdims=True)
    acc_scr[...] = alpha * acc_scr[...] + _dot(p.astype(BF16), v)
    m_scr[...] = m_new


def _online_update_one(s, v, m_scr, l_scr, acc_scr):
    m_old = m_scr[...]
    m_new = jnp.maximum(m_old, s)
    m_safe = jnp.where(m_new == NEG_INF, 0.0, m_new)
    alpha = jnp.exp(m_old - m_safe)
    p = jnp.exp(s - m_safe)
    l_scr[...] = alpha * l_scr[...] + p
    acc_scr[...] = alpha * acc_scr[...] + p.astype(BF16).astype(F32) * v
    m_scr[...] = m_new


def _sample_attn_kernel(pt_ref, lv_ref, g_ref, qa_ref, qs_ref, kan_ref, van_ref, ksn_ref, vsn_ref,
                        msk_ref, mskn_ref, dak_ref, dav_ref, sak_ref, sav_ref, a_ref, o_ref,
                        qda_scr, qsa_scr, m_scr, l_scr, acc_scr, *, n_pages, page, lam_init):
    p = pl.program_id(1)
    past = n_pages * page
    row = lax.broadcasted_iota(jnp.int32, (8, DA_WIDTH), 0)
    col = lax.broadcasted_iota(jnp.int32, (8, DA_WIDTH), 1)
    row1 = lax.broadcasted_iota(jnp.int32, (8, 1), 0)
    slope_da = jnp.zeros((8, 1), F32)
    for h, sl in enumerate(_alibi_slopes(DA_HEADS)):
        slope_da = jnp.where(row1 // 2 == h, sl, slope_da)
    slope_sa = jnp.zeros((8, 1), F32)
    for h, sl in enumerate(_alibi_slopes(SA_HEADS)):
        slope_sa = jnp.where(row1 == h, sl, slope_sa)

    @pl.when(p == 0)
    def _():
        qa = jnp.broadcast_to(qa_ref[...].astype(F32), (8, DA_WIDTH))
        qda_scr[...] = jnp.where(col // DA_HD == row, qa, 0.0)
        qs = jnp.broadcast_to(qs_ref[...].astype(F32), (8, SA_WIDTH))
        qsa_scr[...] = jnp.where(col // SA_HD == row, qs, 0.0)
        m_scr[...] = jnp.full(m_scr.shape, NEG_INF, F32)
        l_scr[...] = jnp.zeros(l_scr.shape, F32)
        acc_scr[...] = jnp.zeros(acc_scr.shape, F32)

    @pl.when(p < n_pages)
    def _():
        kpos = p * page + lax.broadcasted_iota(jnp.int32, (1, page), 1)
        dist = (past - kpos).astype(F32)
        s = _dot_nt(qda_scr[...].astype(BF16), dak_ref[...].astype(BF16)) - slope_da * dist
        _online_update(s, dav_ref[...].astype(BF16), m_scr.at[0], l_scr.at[0], acc_scr.at[0])
        s = _dot_nt(qsa_scr[...].astype(BF16), sak_ref[...].astype(BF16)) - slope_sa * dist + msk_ref[...]
        _online_update(s, sav_ref[...].astype(BF16), m_scr.at[1], l_scr.at[1], acc_scr.at[1])

    @pl.when(p == n_pages)
    def _():
        kn = kan_ref[...].astype(BF16).astype(F32)
        s = jnp.sum(qda_scr[...] * kn, axis=1, keepdims=True)
        _online_update_one(s, van_ref[...].astype(BF16).astype(F32), m_scr.at[0], l_scr.at[0], acc_scr.at[0])
        kn = ksn_ref[...].astype(BF16).astype(F32)
        s = jnp.sum(qsa_scr[...] * kn, axis=1, keepdims=True) + mskn_ref[:, 0:1]
        _online_update_one(s, vsn_ref[...].astype(BF16).astype(F32), m_scr.at[1], l_scr.at[1], acc_scr.at[1])

        lam = _diff_lambda(lv_ref[...], lam_init)
        oda = acc_scr[0] / l_scr[0]
        for h in range(DA_HEADS):
            cs = slice(h * LANES, (h + 1) * LANES)
            o = _diff_combine(oda[2 * h:2 * h + 1, cs], oda[2 * h + 1:2 * h + 2, cs], lam, g_ref[...], lam_init)
            a_ref[:, cs] = o.astype(a_ref.dtype)
        osa = acc_scr[1] / l_scr[1]
        lo, _ = _half_masks((1, LANES))
        for pr in range(SA_HEADS // 2):
            cs = slice(pr * LANES, (pr + 1) * LANES)
            o_ref[:, cs] = jnp.where(lo, osa[2 * pr:2 * pr + 1, cs], osa[2 * pr + 1:2 * pr + 2, cs]).astype(o_ref.dtype)


def _sample_attn(page_table, lv, g, qa, qs, ka_new, va_new, ks_new, vs_new, msk, c_da_k, c_da_v, c_sa_k, c_sa_v,
                 layer, lam_init):
    bd, n_pages = page_table.shape
    page = c_da_k.shape[2]
    tok = lambda w: pl.BlockSpec((None, 1, w), lambda b, p, pt: (b, 0, 0))
    full = lambda a: pl.BlockSpec(a.shape, lambda b, p, pt: (0,) * a.ndim)
    paged = lambda w: pl.BlockSpec((None, None, page, w),
                                   lambda b, p, pt: (layer, pt[b, jnp.minimum(p, n_pages - 1)], 0, 0))
    grid_spec = pltpu.PrefetchScalarGridSpec(
        num_scalar_prefetch=1,
        grid=(bd, n_pages + 1),
        in_specs=[full(lv), full(g), tok(DA_WIDTH), tok(SA_WIDTH), tok(DA_WIDTH), tok(DA_WIDTH),
                  tok(SA_WIDTH), tok(SA_WIDTH),
                  pl.BlockSpec((None, None, 1, page), lambda b, p, pt: (jnp.minimum(p, n_pages - 1), b, 0, 0)),
                  pl.BlockSpec((None, None, 1, page), lambda b, p, pt: (n_pages, b, 0, 0)),
                  paged(DA_WIDTH), paged(DA_WIDTH), paged(SA_WIDTH), paged(SA_WIDTH)],
        out_specs=[tok(DA_WIDTH), tok(SA_WIDTH)],
        scratch_shapes=[pltpu.VMEM((8, DA_WIDTH), F32), pltpu.VMEM((8, SA_WIDTH), F32),
                        pltpu.VMEM((2, 8, 1), F32), pltpu.VMEM((2, 8, 1), F32),
                        pltpu.VMEM((2, 8, DA_WIDTH), F32)],
    )
    return pl.pallas_call(
        functools.partial(_sample_attn_kernel, n_pages=n_pages, page=page, lam_init=lam_init),
        out_shape=[jax.ShapeDtypeStruct((bd, 1, DA_WIDTH), BF16), jax.ShapeDtypeStruct((bd, 1, SA_WIDTH), BF16)],
        grid_spec=grid_spec,
        compiler_params=_params(("parallel", "arbitrary")),
        name="sample_attn",
    )(page_table, lv, g, qa, qs, ka_new, va_new, ks_new, vs_new, msk, msk, c_da_k, c_da_v, c_sa_k, c_sa_v)


def _merge_kernel(a_ref, o_ref, ga_ref, gb_ref, x_ref, wa_ref, wb_ref, wo_ref, g_ref, b_ref, wq_ref,
                  h_ref, q_ref, *, alpha):
    merged = (jax.nn.sigmoid(ga_ref[...]) * _dot(a_ref[...], wa_ref[...])
              + jax.nn.sigmoid(gb_ref[...]) * _dot(o_ref[...], wb_ref[...]))
    h = _layer_norm(alpha * x_ref[...] + _dot(merged.astype(BF16), wo_ref[...]), g_ref[...], b_ref[...])
    h_ref[...] = h
    q_ref[...] = _dot(h.astype(BF16), wq_ref[...]).astype(q_ref.dtype)


def _merge(a, o, ga, gb, x, wa, wb, wo, g, b, wq, alpha):
    t, d_model = x.shape
    tm = min(t, 256)
    row = lambda n: pl.BlockSpec((tm, n), lambda i: (i, 0))
    full = lambda arr: pl.BlockSpec(arr.shape, lambda i: (0,) * arr.ndim)
    nq = wq.shape[1]
    return pl.pallas_call(
        functools.partial(_merge_kernel, alpha=alpha),
        out_shape=[jax.ShapeDtypeStruct((t, d_model), F32), jax.ShapeDtypeStruct((t, nq), BF16)],
        grid=(t // tm,),
        in_specs=[row(DA_WIDTH), row(SA_WIDTH), row(d_model), row(d_model), row(d_model),
                  full(wa), full(wb), full(wo), full(g), full(b), full(wq)],
        out_specs=[row(d_model), row(nq)],
        compiler_params=_params(("parallel",)),
        name="merge_ln_peerq",
    )(a, o, ga, gb, x, wa, wb, wo, g, b, wq)


def _extract_top(s, n_top):
    rows, t = s.shape
    ridx = lax.broadcasted_iota(jnp.int32, (rows, t), 0).astype(F32)
    kidx = lax.broadcasted_iota(jnp.int32, (n_top, t), 0)

    def body(k, carry):
        s, rank, vals = carry
        m = jnp.max(s, axis=0, keepdims=True)
        first = jnp.min(jnp.where(s == m, ridx, float(rows)), axis=0, keepdims=True)
        hit = ridx == first
        s = jnp.where(hit, NEG_INF, s)
        rank = jnp.where(hit, k.astype(F32), rank)
        vals = jnp.where(kidx == k, m, vals)
        return s, rank, vals

    _, rank, vals = lax.fori_loop(
        0, n_top, body, (s, jnp.full((rows, t), float(n_top), F32), jnp.zeros((n_top, t), F32)))
    return vals, rank


def _peer_route_kernel(q_ref, sk_ref, a_ref, n_ref, b_ref, r1_ref):
    half = PEER_QD // 2
    s0 = _dot_nt(sk_ref[0], q_ref[:, :half])
    s1 = _dot_nt(sk_ref[1], q_ref[:, half:])
    v0, r0 = _extract_top(s0, PEER_TOPK)
    v1, r1 = _extract_top(s1, PEER_TOPK)
    cand = jnp.concatenate([v0[a:a + 1, :] + v1 for a in range(PEER_TOPK)], axis=0)
    tv, rc = _extract_top(cand, PEER_TOPK)
    taken = jnp.where(rc < float(PEER_TOPK), 1.0, 0.0)
    z = jnp.sum(jnp.exp(tv - tv[0:1, :]), axis=0, keepdims=True)
    n_sel = jnp.zeros_like(r0)
    for a in range(PEER_TOPK):
        n_a = jnp.sum(taken[a * PEER_TOPK:(a + 1) * PEER_TOPK, :], axis=0, keepdims=True)
        n_sel = jnp.where(r0 == float(a), n_a, n_sel)
    in0 = r0 < float(PEER_TOPK)
    in1 = r1 < float(PEER_TOPK)
    a_ref[...] = jnp.where(in0, jnp.exp(s0 - v0[0:1, :]) / z, 0.0)
    n_ref[...] = n_sel
    b_ref[...] = jnp.where(in1, jnp.exp(s1 - v1[0:1, :]), 0.0)
    r1_ref[...] = r1


def _peer_route(q, sub_keys):
    t = q.shape[0]
    tt = min(t, 256)
    out = jax.ShapeDtypeStruct((PEER_HEADS, PEER_NKEYS, t), F32)
    blk = pl.BlockSpec((None, PEER_NKEYS, tt), lambda i, h: (h, 0, i))
    return pl.pallas_call(
        _peer_route_kernel,
        out_shape=[out] * 4,
        grid=(t // tt, PEER_HEADS),
        in_specs=[pl.BlockSpec((tt, PEER_QD), lambda i, h: (i, h)),
                  pl.BlockSpec((None, 2, PEER_NKEYS, PEER_QD // 2), lambda i, h: (h, 0, 0, 0))],
        out_specs=[blk] * 4,
        compiler_params=_params(("parallel", "arbitrary")),
        name="peer_route",
    )(q, sub_keys)


def _gelu(x):
    return 0.5 * x * (1.0 + lax.erf(x * (2.0 ** -0.5)))


def _peer_expert_kernel(h_ref, u_ref, vt_ref, a_ref, n_ref, b_ref, r1_ref, g_ref, beta_ref, y_ref,
                        hb_scr, pre_scr, act_scr, acc_scr, *, ci, alpha):
    c = pl.program_id(1)

    @pl.when(c == 0)
    def _():
        hb_scr[...] = h_ref[...].astype(BF16)
        acc_scr[...] = jnp.zeros(acc_scr.shape, F32)

    pre_scr[...] = _dot_nt(u_ref[...], hb_scr[...])
    for i in range(ci):
        rows = slice(i * PEER_NKEYS, (i + 1) * PEER_NKEYS)
        w = None
        for hd in range(PEER_HEADS):
            gate = a_ref[hd, i:i + 1, :] * jnp.where(r1_ref[hd] < n_ref[hd, i:i + 1, :], b_ref[hd], 0.0)
            w = gate if w is None else w + gate
        act_scr[rows, :] = (_gelu(pre_scr[rows, :]) * w).astype(BF16)
    acc_scr[...] = acc_scr[...] + _dot(vt_ref[...], act_scr[...])

    @pl.when(c == pl.num_programs(1) - 1)
    def _():
        hv = h_ref[...]
        y_ref[...] = _layer_norm(alpha * hv + acc_scr[...].T, g_ref[...], beta_ref[...])


def _peer_experts(h, u_tab, vt_tab, ra, rn, rb, rr1, g, beta, alpha):
    t, d_model = h.shape
    tt = min(t, 512)
    ci = 8
    ne = ci * PEER_NKEYS
    nch = u_tab.shape[0] // ne
    sml = pl.BlockSpec((PEER_HEADS, ci, tt), lambda i, c: (0, c, i))
    big = pl.BlockSpec((PEER_HEADS, PEER_NKEYS, tt), lambda i, c: (0, 0, i))
    full = lambda arr: pl.BlockSpec(arr.shape, lambda i, c: (0,) * arr.ndim)
    return pl.pallas_call(
        functools.partial(_peer_expert_kernel, ci=ci, alpha=alpha),
        out_shape=jax.ShapeDtypeStruct((t, d_model), F32),
        grid=(t // tt, nch),
        in_specs=[pl.BlockSpec((tt, d_model), lambda i, c: (i, 0)),
                  pl.BlockSpec((ne, d_model), lambda i, c: (c, 0)),
                  pl.BlockSpec((d_model, ne), lambda i, c: (0, c)),
                  sml, sml, big, big, full(g), full(beta)],
        out_specs=pl.BlockSpec((tt, d_model), lambda i, c: (i, 0)),
        scratch_shapes=[pltpu.VMEM((tt, d_model), BF16), pltpu.VMEM((ne, tt), F32),
                        pltpu.VMEM((ne, tt), BF16), pltpu.VMEM((d_model, tt), F32)],
        compiler_params=_params(("parallel", "arbitrary")),
        name="peer_experts",
    )(h, u_tab, vt_tab, ra, rn, rb, rr1, g, beta)


def _layer_tail(x, a, o, ga, gb, tail, alpha):
    wa, wb, wo, ln1_g, ln1_b, wq, sub_keys, u_tab, vt_tab, ln2_g, ln2_b = tail
    h, q = _merge(a, o, ga, gb, x, wa, wb, wo, ln1_g, ln1_b, wq, alpha)
    ra, rn, rb, rr1 = _peer_route(q, sub_keys)
    return _peer_experts(h, u_tab, vt_tab, ra, rn, rb, rr1, ln2_g, ln2_b, alpha)


def kernel(x_prompt, x_sample, cache_da_k, cache_da_v, cache_sa_k, cache_sa_v, cache_idx_k, page_table, w_in, lambda_q1, lambda_k1, lambda_q2, lambda_k2, da_subln_g, w_a_up, w_b_up, w_out, ln1_g, ln1_b, peer_wq, peer_sub_keys, peer_u, peer_v, ln2_g, ln2_b):
    depth = w_in.shape[0]
    bp, sp, d_model = x_prompt.shape
    bd, sd, _ = x_sample.shape
    assert sd == 1, "the sample pass handles one new token per sequence"
    n_pool, page = cache_da_k.shape[1], cache_da_k.shape[2]
    n_pages = page_table.shape[1]
    alpha = (2.0 * depth) ** 0.25
    row2 = lambda v: v.reshape(1, -1)

    y_p = x_prompt.reshape(bp * sp, d_model)
    y_s = x_sample.reshape(bd, d_model)
    p_new, s_new = [], []
    for l in range(depth):
        lam_init = 0.8 - 0.6 * math.exp(-0.3 * l)
        lv = jnp.stack([lambda_q1[l], lambda_k1[l], lambda_q2[l], lambda_k2[l]])
        g_sub = row2(da_subln_g[l])
        w_arr = _arrange_w_in(w_in[l], d_model)
        tail = (w_a_up[l].astype(BF16), w_b_up[l].astype(BF16), w_out[l].astype(BF16), row2(ln1_g[l]),
                row2(ln1_b[l]), peer_wq[l].astype(BF16), peer_sub_keys[l].astype(BF16),
                peer_u[l].astype(BF16), peer_v[l].T.astype(BF16), row2(ln2_g[l]), row2(ln2_b[l]))

        (qa, kaf, kab, vaf, vab, qs, ksf, ksb, vsf, vsb, qi, kif, ki2, wi, ga, gb) = _in_proj(y_p, w_arr)
        r3 = lambda v: v.reshape(bp, sp, v.shape[-1])
        a = _prompt_diff_attn(lv, g_sub, r3(qa), r3(kab), r3(vab), lam_init)
        o = _prompt_sparse_attn(r3(qi), r3(wi), r3(ki2), r3(qs), r3(ksb), r3(vsb))
        p_new.append((kaf.reshape(bp, sp, DA_HEADS, 2, DA_HD), vaf.reshape(bp, sp, DA_HEADS, 2 * DA_HD),
                      ksf.reshape(bp, sp, SA_HEADS, SA_HD), vsf.reshape(bp, sp, SA_HEADS, SA_HD),
                      kif.reshape(bp, sp, IDX_HD)))
        y_p = _layer_tail(y_p, a.reshape(bp * sp, DA_WIDTH), o.reshape(bp * sp, SA_WIDTH), ga, gb, tail, alpha)

        (qa, kaf, kab, vaf, vab, qs, ksf, ksb, vsf, vsb, qi, kif, ki2, wi, ga, gb) = _in_proj(y_s, w_arr)
        t3 = lambda v: v.reshape(bd, 1, v.shape[-1])
        scores = _sample_index_scores(page_table, qi.reshape(bd, IDX_HEADS, IDX_HD),
                                      wi[:, :IDX_HEADS].reshape(bd, IDX_HEADS, 1), t3(kif),
                                      cache_idx_k.reshape(depth, n_pool, page, IDX_HD), l)
        k_sel = min(TOPK_MAX, (n_pages * page + 1) // 4)
        msk = _sample_select(scores.reshape(bd, n_pages + 1, page), k_sel)
        a, o = _sample_attn(page_table, lv, g_sub, t3(qa), t3(qs), t3(kaf), t3(vaf), t3(ksf), t3(vsf),
                            msk.reshape(n_pages + 1, bd, 1, page),
                            cache_da_k.reshape(depth, n_pool, page, DA_WIDTH),
                            cache_da_v.reshape(depth, n_pool, page, DA_WIDTH),
                            cache_sa_k.reshape(depth, n_pool, page, SA_WIDTH),
                            cache_sa_v.reshape(depth, n_pool, page, SA_WIDTH), l, lam_init)
        s_new.append((kaf.reshape(bd, 1, DA_HEADS, 2, DA_HD), vaf.reshape(bd, 1, DA_HEADS, 2 * DA_HD),
                      ksf.reshape(bd, 1, SA_HEADS, SA_HD), vsf.reshape(bd, 1, SA_HEADS, SA_HD),
                      kif.reshape(bd, 1, IDX_HD)))
        y_s = _layer_tail(y_s, a.reshape(bd, DA_WIDTH), o.reshape(bd, SA_WIDTH), ga, gb, tail, alpha)

    stk = lambda lst, j: jnp.stack([e[j] for e in lst])
    return (y_p.reshape(bp, sp, d_model), y_s.reshape(bd, 1, d_model),
            stk(p_new, 0), stk(p_new, 1), stk(p_new, 2), stk(p_new, 3), stk(p_new, 4),
            stk(s_new, 0), stk(s_new, 1), stk(s_new, 2), stk(s_new, 3), stk(s_new, 4))
```

```python
import functools
import math

import jax
import jax.numpy as jnp
from jax import lax
from jax.experimental import pallas as pl
from jax.experimental.pallas import tpu as pltpu

F32 = jnp.float32
BF16 = jnp.bfloat16
NEG_INF = float("-inf")

LANES = 128
SUBLANES = 8
DA_HEADS = 4
DA_HD = 64
DA_WIDTH = DA_HEADS * 2 * DA_HD
SA_HEADS = 8
SA_HD = 64
SA_WIDTH = SA_HEADS * SA_HD
IDX_HEADS = 8
IDX_HD = 64
TOPK_MAX = 256
PEER_HEADS = 8
PEER_NKEYS = 128
PEER_QD = 256
PEER_TOPK = 16
LN_EPS = 1e-5
RMS_EPS = 1e-5
VMEM_LIMIT = 56 * 1024 * 1024


def _alibi_slopes(n):
    return [2.0 ** (-8.0 * (i + 1) / n) for i in range(n)]


def _dot(a, b):
    return jnp.dot(a, b, preferred_element_type=F32)


def _dot_nt(a, b):
    return lax.dot_general(a, b, (((1,), (1,)), ((), ())), preferred_element_type=F32)


def _fold_lanes(x, op):
    parts = [x[:, t * LANES:(t + 1) * LANES] for t in range(x.shape[1] // LANES)]
    return functools.reduce(op, parts)


def _params(sem):
    return pltpu.CompilerParams(dimension_semantics=sem, vmem_limit_bytes=VMEM_LIMIT)


def _layer_norm(x, g, b):
    mu = jnp.mean(x, axis=-1, keepdims=True)
    xc = x - mu
    var = jnp.mean(xc * xc, axis=-1, keepdims=True)
    return xc * lax.rsqrt(var + LN_EPS) * g + b


_SEG = dict(qa=0, ka=512, va=1024, qs=1536, ks=2048, vs=2560, qi=3072, ki=3584, wi=3712, ga=3840)


def _arrange_w_in(w_in, d_model):
    offs = [0]
    for n in (DA_WIDTH, DA_WIDTH, DA_WIDTH, SA_WIDTH, SA_WIDTH, SA_WIDTH,
              IDX_HEADS * IDX_HD, IDX_HD, IDX_HEADS, d_model, d_model):
        offs.append(offs[-1] + n)
    ki = w_in[:, offs[7]:offs[8]]
    wi = w_in[:, offs[8]:offs[9]]
    pad = jnp.zeros((w_in.shape[0], LANES - IDX_HEADS), w_in.dtype)
    return jnp.concatenate([w_in[:, :offs[7]], ki, ki, wi, pad, w_in[:, offs[9]:]], axis=1).astype(BF16)


def _in_proj_kernel(x_ref, w_ref, qa_o, kaf_o, kab_o, vaf_o, vab_o, qs_o, ksf_o, ksb_o, vsf_o, vsb_o,
                    qi_o, kif_o, ki2_o, wi_o, ga_o, gb_o, *, d_model):
    xb = x_ref[...].astype(BF16)

    def seg(name, n):
        a = _SEG[name]
        return _dot(xb, w_ref[:, a:a + n])

    qa_o[...] = (seg("qa", DA_WIDTH) * (DA_HD ** -0.5)).astype(BF16)
    z = seg("ka", DA_WIDTH)
    kaf_o[...] = z
    kab_o[...] = z.astype(BF16)
    z = seg("va", DA_WIDTH)
    vaf_o[...] = z
    vab_o[...] = z.astype(BF16)
    qs_o[...] = (seg("qs", SA_WIDTH) * (SA_HD ** -0.5)).astype(BF16)
    z = seg("ks", SA_WIDTH)
    ksf_o[...] = z
    ksb_o[...] = z.astype(BF16)
    z = seg("vs", SA_WIDTH)
    vsf_o[...] = z
    vsb_o[...] = z.astype(BF16)
    qi_o[...] = (seg("qi", IDX_HEADS * IDX_HD) * (IDX_HD ** -0.5)).astype(BF16)
    z = seg("ki", 2 * IDX_HD)
    kif_o[...] = z[:, :IDX_HD]
    ki2_o[...] = z.astype(BF16)
    wi_o[...] = seg("wi", LANES) * (IDX_HEADS ** -0.5)
    ga_o[...] = seg("ga", d_model)
    gb_o[...] = _dot(xb, w_ref[:, _SEG["ga"] + d_model:_SEG["ga"] + 2 * d_model])


def _in_proj(x, w_arr):
    t, d_model = x.shape
    tm = min(t, 256)
    assert t % tm == 0
    row = lambda n: pl.BlockSpec((tm, n), lambda i: (i, 0))
    widths = [(512, BF16), (512, F32), (512, BF16), (512, F32), (512, BF16), (512, BF16), (512, F32),
              (512, BF16), (512, F32), (512, BF16), (512, BF16), (IDX_HD, F32), (LANES, BF16), (LANES, F32),
              (d_model, F32), (d_model, F32)]
    return pl.pallas_call(
        functools.partial(_in_proj_kernel, d_model=d_model),
        out_shape=[jax.ShapeDtypeStruct((t, n), dt) for n, dt in widths],
        grid=(t // tm,),
        in_specs=[row(d_model), pl.BlockSpec(w_arr.shape, lambda i: (0, 0))],
        out_specs=[row(n) for n, _ in widths],
        compiler_params=_params(("parallel",)),
        name="in_proj",
    )(x, w_arr)


def _sortable_key(x):
    bits = lax.bitcast_convert_type(x + 0.0, jnp.int32)
    return bits ^ ((bits >> 31) & jnp.int32(0x7FFFFFFF))


def _topk_mask(key_scr, msk_scr, nk, k_sel, cw, idx_bits):
    rows = key_scr.shape[1]
    pos = lax.broadcasted_iota(jnp.int32, (rows, cw), 1)

    def count(pred_fn):
        def body(c, acc):
            hit = jnp.where(pred_fn(key_scr[c], pos + c * cw), 1.0, 0.0)
            return acc + _fold_lanes(hit, jnp.add)
        acc = lax.fori_loop(0, nk, body, jnp.zeros((rows, LANES), F32))
        return jnp.sum(acc, axis=1, keepdims=True)

    def thr_bit(it, thr):
        cand = thr + lax.shift_left(jnp.int32(1), 31 - it)
        cnt = count(lambda key, p: key >= cand)
        return jnp.where(cnt >= k_sel, cand, thr)

    thr = lax.fori_loop(0, 32, thr_bit, jnp.full((rows, 1), jnp.iinfo(jnp.int32).min, jnp.int32))
    need = k_sel - count(lambda key, p: key > thr)

    def lim_bit(it, lim):
        cand = lim + lax.shift_left(jnp.int32(1), idx_bits - 1 - it)
        cnt = count(lambda key, p: (key == thr) & (p < cand))
        return jnp.where(cnt < need, cand, lim)

    lim = lax.fori_loop(0, idx_bits, lim_bit, jnp.zeros((rows, 1), jnp.int32))

    def write(c, carry):
        key = key_scr[c]
        sel = (key > thr) | ((key == thr) & (pos + c * cw <= lim))
        msk_scr[c] = jnp.where(sel, 0.0, NEG_INF)
        return carry

    lax.fori_loop(0, nk, write, 0)


def _attend(units, q_of, kcols, vcols, slopes, k_ref, v_ref, msk_scr, s_scr, m_scr, l_scr, acc_scr,
            q0, nk, sq, ck):
    rel = (lax.broadcasted_iota(jnp.int32, (sq, ck), 0)
           - lax.broadcasted_iota(jnp.int32, (sq, ck), 1)).astype(F32)
    m_scr[...] = jnp.full(m_scr.shape, NEG_INF, F32)
    l_scr[...] = jnp.zeros(l_scr.shape, F32)
    acc_scr[...] = jnp.zeros(acc_scr.shape, F32)

    def pass1(kc, carry):
        k0 = pl.multiple_of(kc * ck, ck)
        dist = (q0 - k0).astype(F32) + rel
        ok = dist >= 0.0
        extra = None if msk_scr is None else msk_scr[kc]
        for u in units:
            kb = k_ref[pl.ds(k0, ck), kcols(u)]
            s = _dot_nt(q_of(u), kb) - slopes[u] * dist
            if extra is not None:
                s = s + extra
            s = jnp.where(ok, s, NEG_INF)
            s_scr[u, kc] = s
            m_scr[u] = jnp.maximum(m_scr[u], _fold_lanes(s, jnp.maximum))
        return carry

    lax.fori_loop(0, nk, pass1, 0)
    m_col = [jnp.max(m_scr[u], axis=1, keepdims=True) for u in units]

    def pass2(kc, carry):
        k0 = pl.multiple_of(kc * ck, ck)
        for u in units:
            p = jnp.exp(s_scr[u, kc] - m_col[u])
            l_scr[u] = l_scr[u] + _fold_lanes(p, jnp.add)
            acc_scr[u] = acc_scr[u] + _dot(p.astype(BF16), v_ref[pl.ds(k0, ck), vcols(u)])
        return carry

    lax.fori_loop(0, nk, pass2, 0)


def _half_masks(shape):
    lane = lax.broadcasted_iota(jnp.int32, shape, 1)
    return lane < (LANES // 2), lane >= (LANES // 2)


def _diff_lambda(lv, lam_init):
    d1 = jnp.sum(lv[0:1, :] * lv[1:2, :], axis=1, keepdims=True)
    d2 = jnp.sum(lv[2:3, :] * lv[3:4, :], axis=1, keepdims=True)
    return jnp.exp(d1) - jnp.exp(d2) + lam_init


def _diff_combine(o0, o1, lam, g, lam_init):
    o = o0 - lam * o1
    o = o * lax.rsqrt(jnp.mean(o * o, axis=-1, keepdims=True) + RMS_EPS)
    return o * g * (1.0 - lam_init)


def _diff_attn_kernel(lv_ref, g_ref, q_ref, k_ref, v_ref, o_ref, s_scr, m_scr, l_scr, acc_scr,
                      *, sq, ck, lam_init):
    j = pl.program_id(1)
    q0 = j * sq
    nk = (q0 + sq - 1) // ck + 1
    lo, hi = _half_masks((sq, LANES))
    qm = []
    for h in range(DA_HEADS):
        qh = q_ref[:, h * LANES:(h + 1) * LANES]
        qm.append(jnp.where(lo, qh, jnp.zeros_like(qh)))
        qm.append(jnp.where(hi, qh, jnp.zeros_like(qh)))
    units = list(range(2 * DA_HEADS))
    cols = lambda u: slice((u // 2) * LANES, (u // 2 + 1) * LANES)
    slopes = [s for s in _alibi_slopes(DA_HEADS) for _ in range(2)]
    _attend(units, lambda u: qm[u], cols, cols, slopes, k_ref, v_ref, None, s_scr, m_scr, l_scr, acc_scr,
            q0, nk, sq, ck)
    lam = _diff_lambda(lv_ref[...], lam_init)
    for h in range(DA_HEADS):
        o0 = acc_scr[2 * h] / jnp.sum(l_scr[2 * h], axis=1, keepdims=True)
        o1 = acc_scr[2 * h + 1] / jnp.sum(l_scr[2 * h + 1], axis=1, keepdims=True)
        o = _diff_combine(o0, o1, lam, g_ref[...], lam_init)
        o_ref[:, h * LANES:(h + 1) * LANES] = o.astype(o_ref.dtype)


def _prompt_diff_attn(lv, g, qa, ka, va, lam_init):
    b, s, _ = qa.shape
    sq = min(s, 128)
    ck = min(s, 512)
    nc = s // ck
    u = 2 * DA_HEADS
    blk_q = pl.BlockSpec((None, sq, DA_WIDTH), lambda bi, j: (bi, j, 0))
    blk_kv = pl.BlockSpec((None, s, DA_WIDTH), lambda bi, j: (bi, 0, 0))
    full = lambda a: pl.BlockSpec(a.shape, lambda bi, j: (0,) * a.ndim)
    return pl.pallas_call(
        functools.partial(_diff_attn_kernel, sq=sq, ck=ck, lam_init=lam_init),
        out_shape=jax.ShapeDtypeStruct((b, s, DA_WIDTH), BF16),
        grid=(b, s // sq),
        in_specs=[full(lv), full(g), blk_q, blk_kv, blk_kv],
        out_specs=blk_q,
        scratch_shapes=[pltpu.VMEM((u, nc, sq, ck), F32), pltpu.VMEM((u, sq, LANES), F32),
                        pltpu.VMEM((u, sq, LANES), F32), pltpu.VMEM((u, sq, LANES), F32)],
        compiler_params=_params(("parallel", "arbitrary")),
        name="prompt_diff_attn",
    )(lv, g, qa, ka, va)


def _index_scores_chunk(qim, wi, kblk):
    total = None
    for h in range(IDX_HEADS):
        r = jnp.maximum(_dot_nt(qim[h], kblk), 0.0)
        term = wi[:, h:h + 1] * r
        total = term if total is None else total + term
    return total


def _sparse_attn_kernel(qi_ref, wi_ref, ki_ref, qs_ref, ks_ref, vs_ref, o_ref,
                        key_scr, msk_scr, s_scr, m_scr, l_scr, acc_scr, *, sq, ck, k_sel, idx_bits):
    j = pl.program_id(1)
    q0 = j * sq
    nk = (q0 + sq - 1) // ck + 1
    lo, hi = _half_masks((sq, LANES))

    def masked(ref):
        out = []
        for p in range(ref.shape[1] // LANES):
            blk = ref[:, p * LANES:(p + 1) * LANES]
            out.append(jnp.where(lo, blk, jnp.zeros_like(blk)))
            out.append(jnp.where(hi, blk, jnp.zeros_like(blk)))
        return out

    qim = masked(qi_ref)
    wi = wi_ref[...]
    rel = (lax.broadcasted_iota(jnp.int32, (sq, ck), 0) - lax.broadcasted_iota(jnp.int32, (sq, ck), 1))

    def scores(kc, carry):
        k0 = pl.multiple_of(kc * ck, ck)
        tot = _index_scores_chunk(qim, wi, ki_ref[pl.ds(k0, ck), :])
        valid = (q0 - k0) + rel >= 0
        key_scr[kc] = _sortable_key(jnp.where(valid, tot, NEG_INF))
        return carry

    lax.fori_loop(0, nk, scores, 0)
    _topk_mask(key_scr, msk_scr, nk, k_sel, ck, idx_bits)

    qsm = masked(qs_ref)
    units = list(range(SA_HEADS))
    cols = lambda u: slice((u // 2) * LANES, (u // 2 + 1) * LANES)
    _attend(units, lambda u: qsm[u], cols, cols, _alibi_slopes(SA_HEADS), ks_ref, vs_ref, msk_scr,
            s_scr, m_scr, l_scr, acc_scr, q0, nk, sq, ck)
    for p in range(SA_HEADS // 2):
        oa = acc_scr[2 * p] / jnp.sum(l_scr[2 * p], axis=1, keepdims=True)
        ob = acc_scr[2 * p + 1] / jnp.sum(l_scr[2 * p + 1], axis=1, keepdims=True)
        o_ref[:, p * LANES:(p + 1) * LANES] = jnp.where(lo, oa, ob).astype(o_ref.dtype)


def _prompt_sparse_attn(qi, wi, ki2, qs, ks, vs):
    b, s, _ = qs.shape
    sq = min(s, 128)
    ck = min(s, 512)
    nc = s // ck
    k_sel = min(TOPK_MAX, s // 4)
    idx_bits = max(1, (s - 1).bit_length())
    blk_q = lambda w: pl.BlockSpec((None, sq, w), lambda bi, j: (bi, j, 0))
    blk_kv = lambda w: pl.BlockSpec((None, s, w), lambda bi, j: (bi, 0, 0))
    u = SA_HEADS
    return pl.pallas_call(
        functools.partial(_sparse_attn_kernel, sq=sq, ck=ck, k_sel=k_sel, idx_bits=idx_bits),
        out_shape=jax.ShapeDtypeStruct((b, s, SA_WIDTH), BF16),
        grid=(b, s // sq),
        in_specs=[blk_q(IDX_HEADS * IDX_HD), blk_q(LANES), blk_kv(LANES),
                  blk_q(SA_WIDTH), blk_kv(SA_WIDTH), blk_kv(SA_WIDTH)],
        out_specs=blk_q(SA_WIDTH),
        scratch_shapes=[pltpu.VMEM((nc, sq, ck), jnp.int32), pltpu.VMEM((nc, sq, ck), F32),
                        pltpu.VMEM((u, nc, sq, ck), F32), pltpu.VMEM((u, sq, LANES), F32),
                        pltpu.VMEM((u, sq, LANES), F32), pltpu.VMEM((u, sq, LANES), F32)],
        compiler_params=_params(("parallel", "arbitrary")),
        name="prompt_sparse_attn",
    )(qi, wi, ki2, qs, ks, vs)


def _sample_index_kernel(pt_ref, qi_ref, wi_ref, kin_ref, cache_ref, o_ref, buf, sem,
                         *, layer, sb, n_pages, page):
    g = pl.program_id(0)
    ng = pl.num_programs(0)
    past = n_pages * page

    def copies(step, slot):
        out = []
        for s in range(sb):
            for p in range(n_pages):
                src = cache_ref.at[layer, pt_ref[step * sb + s, p]]
                out.append(pltpu.make_async_copy(src, buf.at[slot, s, :, pl.ds(p * page, page)], sem.at[slot]))
        return out

    slot = g % 2

    @pl.when(g == 0)
    def _():
        for c in copies(0, 0):
            c.start()

    @pl.when(g + 1 < ng)
    def _():
        for c in copies(g + 1, 1 - slot):
            c.start()

    for c in copies(g, slot):
        c.wait()

    lane = lax.broadcasted_iota(jnp.int32, (1, page), 1)
    for s in range(sb):
        q = qi_ref[s]
        w = wi_ref[s]
        r = jnp.maximum(_dot(q, buf[slot, s].astype(BF16)), 0.0)
        o_ref[s:s + 1, 0:past] = jnp.sum(w * r, axis=0, keepdims=True)
        kn = kin_ref[s].astype(BF16).astype(F32)
        rn = jnp.maximum(jnp.sum(q.astype(F32) * kn, axis=1, keepdims=True), 0.0)
        tot = jnp.sum(w * rn, axis=0, keepdims=True)
        o_ref[s:s + 1, past:past + page] = jnp.where(lane == 0, tot, NEG_INF)


def _sample_index_scores(page_table, qi8, wi8, ki_new, idx_kt, layer):
    bd, n_pages = page_table.shape
    page = idx_kt.shape[3]
    sb = min(bd, SUBLANES)
    assert bd % sb == 0
    lpad = (n_pages + 1) * page
    grid_spec = pltpu.PrefetchScalarGridSpec(
        num_scalar_prefetch=1,
        grid=(bd // sb,),
        in_specs=[
            pl.BlockSpec((sb, IDX_HEADS, IDX_HD), lambda g, pt: (g, 0, 0)),
            pl.BlockSpec((sb, IDX_HEADS, 1), lambda g, pt: (g, 0, 0)),
            pl.BlockSpec((sb, 1, IDX_HD), lambda g, pt: (g, 0, 0)),
            pl.BlockSpec(memory_space=pl.ANY),
        ],
        out_specs=pl.BlockSpec((sb, lpad), lambda g, pt: (g, 0)),
        scratch_shapes=[pltpu.VMEM((2, sb, IDX_HD, n_pages * page), F32), pltpu.SemaphoreType.DMA((2,))],
    )
    return pl.pallas_call(
        functools.partial(_sample_index_kernel, layer=layer, sb=sb, n_pages=n_pages, page=page),
        out_shape=jax.ShapeDtypeStruct((bd, lpad), F32),
        grid_spec=grid_spec,
        compiler_params=_params(("arbitrary",)),
        name="sample_index_scores",
    )(page_table, qi8, wi8, ki_new, idx_kt)


def _sample_select_kernel(sc_ref, msk_ref, key_scr, msk_scr, *, k_sel, cw, idx_bits):
    nc = key_scr.shape[0]
    for c in range(nc):
        key_scr[c] = _sortable_key(sc_ref[:, c * cw:(c + 1) * cw])
    _topk_mask(key_scr, msk_scr, nc, k_sel, cw, idx_bits)
    for c in range(nc):
        msk_ref[:, c * cw:(c + 1) * cw] = msk_scr[c]


def _sample_select(scores, k_sel, cw):
    bd, lpad = scores.shape
    nc = lpad // cw
    idx_bits = max(1, (lpad - 1).bit_length())
    return pl.pallas_call(
        functools.partial(_sample_select_kernel, k_sel=k_sel, cw=cw, idx_bits=idx_bits),
        out_shape=jax.ShapeDtypeStruct((bd, lpad), F32),
        grid=(1,),
        in_specs=[pl.BlockSpec((bd, lpad), lambda i: (0, 0))],
        out_specs=pl.BlockSpec((bd, lpad), lambda i: (0, 0)),
        scratch_shapes=[pltpu.VMEM((nc, bd, cw), jnp.int32), pltpu.VMEM((nc, bd, cw), F32)],
        compiler_params=_params(("arbitrary",)),
        name="sample_select",
    )(scores)


def _online_update(s, pv_fn, m_scr, l_scr, acc_scr):
    m_old = m_scr[...]
    m_new = jnp.maximum(m_old, jnp.max(s, axis=1, keepdims=True))
    m_safe = jnp.where(m_new == NEG_INF, 0.0, m_new)
    alpha = jnp.exp(m_old - m_safe)
    p = jnp.exp(s - m_safe)
    l_scr[...] = alpha * l_scr[...] + jnp.sum(p, axis=1, keepdims=True)
    acc_scr[...] = alpha * acc_scr[...] + pv_fn(p.astype(BF16))
    m_scr[...] = m_new


def _sample_attn_kernel(pt_ref, lv_ref, g_ref, qa_ref, qs_ref, kan_ref, van_ref, ksn_ref, vsn_ref,
                        msk_ref, mskn_ref, dak_hbm, dav_hbm, sak_hbm, sav_hbm, a_ref, o_ref,
                        dak_buf, dav_buf, sak_buf, sav_buf, sem, qda_scr, qsa_scr, m_scr, l_scr, acc_scr,
                        *, layer, n_pages, pg, page, lam_init):
    b = pl.program_id(0)
    hf = pl.program_id(1)
    nh = pl.num_programs(1)
    step = b * nh + hf
    n_steps = pl.num_programs(0) * nh
    slot = step % 2
    past = n_pages * page
    span = pg * page

    def copies(bb, hh, sl):
        out = []
        for j in range(pg):
            pid = pt_ref[bb, hh * pg + j]
            cols = pl.ds(j * page, page)
            out.append(pltpu.make_async_copy(dak_hbm.at[layer, pid], dak_buf.at[sl, :, cols], sem.at[sl, 0]))
            out.append(pltpu.make_async_copy(dav_hbm.at[layer, pid],
                                             dav_buf.at[sl, pl.ds(j * page * DA_HEADS, page * DA_HEADS), :],
                                             sem.at[sl, 1]))
            out.append(pltpu.make_async_copy(sak_hbm.at[layer, pid], sak_buf.at[sl, :, cols], sem.at[sl, 2]))
            out.append(pltpu.make_async_copy(sav_hbm.at[layer, pid], sav_buf.at[sl, :, cols], sem.at[sl, 3]))
        return out

    @pl.when(step == 0)
    def _():
        for c in copies(0, 0, 0):
            c.start()

    @pl.when(step + 1 < n_steps)
    def _():
        wrap = hf + 1 == nh
        for c in copies(jnp.where(wrap, b + 1, b), jnp.where(wrap, 0, hf + 1), 1 - slot):
            c.start()

    row = lax.broadcasted_iota(jnp.int32, (SUBLANES, DA_WIDTH), 0)
    col = lax.broadcasted_iota(jnp.int32, (SUBLANES, DA_WIDTH), 1)
    row1 = lax.broadcasted_iota(jnp.int32, (SUBLANES, 1), 0)
    slope_da = jnp.zeros((SUBLANES, 1), F32)
    for h, sl in enumerate(_alibi_slopes(DA_HEADS)):
        slope_da = jnp.where(row1 // 2 == h, sl, slope_da)
    slope_sa = jnp.zeros((SUBLANES, 1), F32)
    for h, sl in enumerate(_alibi_slopes(SA_HEADS)):
        slope_sa = jnp.where(row1 == h, sl, slope_sa)

    @pl.when(hf == 0)
    def _():
        qa = jnp.broadcast_to(qa_ref[...].astype(F32), (SUBLANES, DA_WIDTH))
        qda_scr[...] = jnp.where(col // DA_HD == row, qa, 0.0)
        qs = jnp.broadcast_to(qs_ref[...].astype(F32), (SUBLANES, SA_WIDTH))
        qsa_scr[...] = jnp.where(col // SA_HD == row, qs, 0.0)
        m_scr[...] = jnp.full(m_scr.shape, NEG_INF, F32)
        l_scr[...] = jnp.zeros(l_scr.shape, F32)
        acc_scr[...] = jnp.zeros(acc_scr.shape, F32)

    for c in copies(b, hf, slot):
        c.wait()

    kpos = hf * span + lax.broadcasted_iota(jnp.int32, (1, span), 1)
    dist = (past - kpos).astype(F32)

    def pv_da(p):
        return jnp.concatenate(
            [_dot(p, dav_buf[slot, pl.ds(h, span, stride=DA_HEADS), :].astype(BF16)) for h in range(DA_HEADS)],
            axis=1)

    s = _dot(qda_scr[...].astype(BF16), dak_buf[slot].astype(BF16)) - slope_da * dist
    _online_update(s, pv_da, m_scr.at[0], l_scr.at[0], acc_scr.at[0])
    s = _dot(qsa_scr[...].astype(BF16), sak_buf[slot].astype(BF16)) - slope_sa * dist + msk_ref[...]
    _online_update(s, lambda p: _dot_nt(p, sav_buf[slot].astype(BF16)), m_scr.at[1], l_scr.at[1], acc_scr.at[1])

    @pl.when(hf == nh - 1)
    def _():
        rnd = lambda ref: ref[...].astype(BF16).astype(F32)
        s = jnp.sum(qda_scr[...] * rnd(kan_ref), axis=1, keepdims=True)
        _online_update(s, lambda p: p.astype(F32) * rnd(van_ref), m_scr.at[0], l_scr.at[0], acc_scr.at[0])
        s = jnp.sum(qsa_scr[...] * rnd(ksn_ref), axis=1, keepdims=True) + mskn_ref[:, 0:1]
        _online_update(s, lambda p: p.astype(F32) * rnd(vsn_ref), m_scr.at[1], l_scr.at[1], acc_scr.at[1])

        lam = _diff_lambda(lv_ref[...], lam_init)
        oda = acc_scr[0] / l_scr[0]
        for h in range(DA_HEADS):
            cs = slice(h * LANES, (h + 1) * LANES)
            o = _diff_combine(oda[2 * h:2 * h + 1, cs], oda[2 * h + 1:2 * h + 2, cs], lam, g_ref[...], lam_init)
            a_ref[:, cs] = o.astype(a_ref.dtype)
        osa = acc_scr[1] / l_scr[1]
        lo, _ = _half_masks((1, LANES))
        for pr in range(SA_HEADS // 2):
            cs = slice(pr * LANES, (pr + 1) * LANES)
            o_ref[:, cs] = jnp.where(lo, osa[2 * pr:2 * pr + 1, cs], osa[2 * pr + 1:2 * pr + 2, cs]).astype(o_ref.dtype)


def _sample_attn(page_table, lv, g, qa, qs, ka_new, va_new, ks_new, vs_new, msk, da_kt, da_v, sa_kt, sa_vt,
                 layer, lam_init):
    bd, n_pages = page_table.shape
    page = da_kt.shape[3]
    pg = math.gcd(n_pages, 8)
    nh = n_pages // pg
    span = pg * page
    tok = lambda w: pl.BlockSpec((None, 1, w), lambda b, hf, pt: (b, 0, 0))
    full = lambda a: pl.BlockSpec(a.shape, lambda b, hf, pt: (0,) * a.ndim)
    hbm = pl.BlockSpec(memory_space=pl.ANY)
    grid_spec = pltpu.PrefetchScalarGridSpec(
        num_scalar_prefetch=1,
        grid=(bd, nh),
        in_specs=[full(lv), full(g), tok(DA_WIDTH), tok(SA_WIDTH), tok(DA_WIDTH), tok(DA_WIDTH),
                  tok(SA_WIDTH), tok(SA_WIDTH),
                  pl.BlockSpec((None, 1, span), lambda b, hf, pt: (b, 0, hf)),
                  pl.BlockSpec((None, 1, page), lambda b, hf, pt: (b, 0, n_pages)),
                  hbm, hbm, hbm, hbm],
        out_specs=[tok(DA_WIDTH), tok(SA_WIDTH)],
        scratch_shapes=[pltpu.VMEM((2, DA_WIDTH, span), F32), pltpu.VMEM((2, span * DA_HEADS, LANES), F32),
                        pltpu.VMEM((2, SA_WIDTH, span), F32), pltpu.VMEM((2, SA_WIDTH, span), F32),
                        pltpu.SemaphoreType.DMA((2, 4)),
                        pltpu.VMEM((SUBLANES, DA_WIDTH), F32), pltpu.VMEM((SUBLANES, SA_WIDTH), F32),
                        pltpu.VMEM((2, SUBLANES, 1), F32), pltpu.VMEM((2, SUBLANES, 1), F32),
                        pltpu.VMEM((2, SUBLANES, DA_WIDTH), F32)],
    )
    return pl.pallas_call(
        functools.partial(_sample_attn_kernel, layer=layer, n_pages=n_pages, pg=pg, page=page, lam_init=lam_init),
        out_shape=[jax.ShapeDtypeStruct((bd, 1, DA_WIDTH), BF16), jax.ShapeDtypeStruct((bd, 1, SA_WIDTH), BF16)],
        grid_spec=grid_spec,
        compiler_params=_params(("arbitrary", "arbitrary")),
        name="sample_attn",
    )(page_table, lv, g, qa, qs, ka_new, va_new, ks_new, vs_new, msk, msk, da_kt, da_v, sa_kt, sa_vt)


def _merge_kernel(a_ref, o_ref, ga_ref, gb_ref, x_ref, wa_ref, wb_ref, wo_ref, g_ref, b_ref, wq_ref,
                  h_ref, q_ref, *, alpha):
    merged = (jax.nn.sigmoid(ga_ref[...]) * _dot(a_ref[...], wa_ref[...])
              + jax.nn.sigmoid(gb_ref[...]) * _dot(o_ref[...], wb_ref[...]))
    h = _layer_norm(alpha * x_ref[...] + _dot(merged.astype(BF16), wo_ref[...]), g_ref[...], b_ref[...])
    h_ref[...] = h
    q = _dot(h.astype(BF16), wq_ref[...]).astype(q_ref.dtype)
    for hd in range(PEER_HEADS):
        q_ref[hd] = q[:, hd * PEER_QD:(hd + 1) * PEER_QD]


def _merge(a, o, ga, gb, x, wa, wb, wo, g, b, wq, alpha):
    t, d_model = x.shape
    tm = min(t, 256)
    row = lambda n: pl.BlockSpec((tm, n), lambda i: (i, 0))
    full = lambda arr: pl.BlockSpec(arr.shape, lambda i: (0,) * arr.ndim)
    return pl.pallas_call(
        functools.partial(_merge_kernel, alpha=alpha),
        out_shape=[jax.ShapeDtypeStruct((t, d_model), F32), jax.ShapeDtypeStruct((PEER_HEADS, t, PEER_QD), BF16)],
        grid=(t // tm,),
        in_specs=[row(DA_WIDTH), row(SA_WIDTH), row(d_model), row(d_model), row(d_model),
                  full(wa), full(wb), full(wo), full(g), full(b), full(wq)],
        out_specs=[row(d_model), pl.BlockSpec((PEER_HEADS, tm, PEER_QD), lambda i: (0, i, 0))],
        compiler_params=_params(("parallel",)),
        name="merge_ln_peerq",
    )(a, o, ga, gb, x, wa, wb, wo, g, b, wq)


def _take_max(s, ridx):
    m = jnp.max(s, axis=0, keepdims=True)
    first = jnp.min(jnp.where(s == m, ridx, float(s.shape[0])), axis=0, keepdims=True)
    return m, ridx == first


_CAND_LIM = [PEER_TOPK // (a + 1) for a in range(SUBLANES)]


def _peer_route_head(q0, q1, sk0, sk1, r0_scr, r1_scr):
    topk = PEER_TOPK
    s0 = _dot_nt(sk0, q0)
    s1 = _dot_nt(sk1, q1)
    rows, t = s0.shape
    ridx = lax.broadcasted_iota(jnp.int32, (rows, t), 0).astype(F32)
    kidx = lax.broadcasted_iota(jnp.int32, (topk, t), 0)
    r0_scr[...] = jnp.full((rows, t), float(topk), F32)
    r1_scr[...] = jnp.full((rows, t), float(topk), F32)

    def body(k, carry):
        c0, c1, v0, v1 = carry
        kf = jnp.asarray(k).astype(F32)
        m0, hit0 = _take_max(c0, ridx)
        m1, hit1 = _take_max(c1, ridx)
        r0_scr[...] = jnp.where(hit0, kf, r0_scr[...])
        r1_scr[...] = jnp.where(hit1, kf, r1_scr[...])
        return (jnp.where(hit0, NEG_INF, c0), jnp.where(hit1, NEG_INF, c1),
                jnp.where(kidx == k, m0, v0), jnp.where(kidx == k, m1, v1))

    zero = jnp.zeros((topk, t), F32)
    _, _, v0, v1 = lax.fori_loop(0, topk, body, (s0, s1, zero, zero))

    sub = lax.broadcasted_iota(jnp.int32, (SUBLANES, t), 0)
    groups = [v0[0:1, :] + v1]
    for a in range(1, SUBLANES):
        groups.append(jnp.where(sub < _CAND_LIM[a], v0[a:a + 1, :] + v1[0:SUBLANES, :], NEG_INF))
    groups.append(v0[SUBLANES:topk, :] + v1[0:1, :])
    cand = jnp.concatenate(groups, axis=0)
    crows = cand.shape[0]
    cidx = lax.broadcasted_iota(jnp.int32, (crows, t), 0).astype(F32)

    def cbody(k, carry):
        c, taken, tv = carry
        m, hit = _take_max(c, cidx)
        return jnp.where(hit, NEG_INF, c), jnp.where(hit, 1.0, taken), jnp.where(kidx == k, m, tv)

    _, taken, tv = lax.fori_loop(0, topk, cbody, (cand, jnp.zeros((crows, t), F32), zero))
    z = jnp.sum(jnp.exp(tv - tv[0:1, :]), axis=0, keepdims=True)

    r0 = r0_scr[...]
    r1 = r1_scr[...]
    n_sel = jnp.zeros((rows, t), F32)
    for a in range(topk):
        if a == 0:
            n_a = jnp.sum(taken[0:topk, :], axis=0, keepdims=True)
        elif a < SUBLANES:
            base = topk + SUBLANES * (a - 1)
            n_a = jnp.sum(taken[base:base + SUBLANES, :], axis=0, keepdims=True)
        else:
            base = topk + SUBLANES * (SUBLANES - 1) + (a - SUBLANES)
            n_a = taken[base:base + 1, :]
        n_sel = jnp.where(r0 == float(a), n_a, n_sel)
    a_out = jnp.where(r0 < float(topk), jnp.exp(s0 - v0[0:1, :]) / z, 0.0)
    b_out = jnp.where(r1 < float(topk), jnp.exp(s1 - v1[0:1, :]), 0.0)
    return a_out, n_sel, b_out, r1


def _peer_route_kernel(q_ref, sk_ref, a_ref, n_ref, b_ref, r1_ref, r0_scr, r1_scr):
    half = PEER_QD // 2

    def head(h, carry):
        a, n, b, r1 = _peer_route_head(q_ref[h, :, :half], q_ref[h, :, half:],
                                       sk_ref[h, 0], sk_ref[h, 1], r0_scr, r1_scr)
        a_ref[h] = a
        n_ref[h] = n
        b_ref[h] = b.astype(b_ref.dtype)
        r1_ref[h] = r1.astype(r1_ref.dtype)
        return carry

    lax.fori_loop(0, PEER_HEADS, head, 0)


def _peer_route(q, sub_keys):
    t = q.shape[1]
    tt = min(t, LANES)
    blk = pl.BlockSpec((PEER_HEADS, PEER_NKEYS, tt), lambda i: (0, 0, i))
    shp = lambda dt: jax.ShapeDtypeStruct((PEER_HEADS, PEER_NKEYS, t), dt)
    return pl.pallas_call(
        _peer_route_kernel,
        out_shape=[shp(F32), shp(F32), shp(BF16), shp(BF16)],
        grid=(t // tt,),
        in_specs=[pl.BlockSpec((PEER_HEADS, tt, PEER_QD), lambda i: (0, i, 0)),
                  pl.BlockSpec(sub_keys.shape, lambda i: (0, 0, 0, 0))],
        out_specs=[blk] * 4,
        scratch_shapes=[pltpu.VMEM((PEER_NKEYS, tt), F32), pltpu.VMEM((PEER_NKEYS, tt), F32)],
        compiler_params=_params(("parallel",)),
        name="peer_route",
    )(q, sub_keys)


def _gelu(x):
    return 0.5 * x * (1.0 + lax.erf(x * (2.0 ** -0.5)))


def _peer_expert_kernel(h_ref, u_ref, vt_ref, a_ref, n_ref, b_ref, r1_ref, g_ref, beta_ref, y_ref,
                        hb_scr, pre_scr, act_scr, acc_scr, *, ci, alpha):
    c = pl.program_id(1)

    @pl.when(c == 0)
    def _():
        hb_scr[...] = h_ref[...].astype(BF16)
        acc_scr[...] = jnp.zeros(acc_scr.shape, F32)

    pre_scr[...] = _dot_nt(u_ref[...], hb_scr[...])
    for i in range(ci):
        rows = slice(i * PEER_NKEYS, (i + 1) * PEER_NKEYS)
        w = None
        for hd in range(PEER_HEADS):
            a_row = a_ref[hd, i:i + 1, :].astype(BF16)
            n_row = n_ref[hd, i:i + 1, :].astype(BF16)
            b = b_ref[hd]
            gate = a_row * jnp.where(r1_ref[hd] < n_row, b, jnp.zeros_like(b))
            w = gate if w is None else w + gate
        act_scr[rows, :] = _gelu(pre_scr[rows, :]).astype(BF16) * w
    acc_scr[...] = acc_scr[...] + _dot(vt_ref[...], act_scr[...])

    @pl.when(c == pl.num_programs(1) - 1)
    def _():
        hv = h_ref[...]
        y_ref[...] = _layer_norm(alpha * hv + acc_scr[...].T, g_ref[...], beta_ref[...])


def _peer_experts(h, u_tab, vt_tab, ra, rn, rb, rr1, g, beta, alpha):
    t, d_model = h.shape
    tt = min(t, 512)
    ci = SUBLANES
    ne = ci * PEER_NKEYS
    nch = u_tab.shape[0] // ne
    sml = pl.BlockSpec((PEER_HEADS, ci, tt), lambda i, c: (0, c, i))
    big = pl.BlockSpec((PEER_HEADS, PEER_NKEYS, tt), lambda i, c: (0, 0, i))
    full = lambda arr: pl.BlockSpec(arr.shape, lambda i, c: (0,) * arr.ndim)
    return pl.pallas_call(
        functools.partial(_peer_expert_kernel, ci=ci, alpha=alpha),
        out_shape=jax.ShapeDtypeStruct((t, d_model), F32),
        grid=(t // tt, nch),
        in_specs=[pl.BlockSpec((tt, d_model), lambda i, c: (i, 0)),
                  pl.BlockSpec((ne, d_model), lambda i, c: (c, 0)),
                  pl.BlockSpec((d_model, ne), lambda i, c: (0, c)),
                  sml, sml, big, big, full(g), full(beta)],
        out_specs=pl.BlockSpec((tt, d_model), lambda i, c: (i, 0)),
        scratch_shapes=[pltpu.VMEM((tt, d_model), BF16), pltpu.VMEM((ne, tt), F32),
                        pltpu.VMEM((ne, tt), BF16), pltpu.VMEM((d_model, tt), F32)],
        compiler_params=_params(("parallel", "arbitrary")),
        name="peer_experts",
    )(h, u_tab, vt_tab, ra, rn, rb, rr1, g, beta)


def _layer_tail(x, a, o, ga, gb, tail, alpha):
    wa, wb, wo, ln1_g, ln1_b, wq, sub_keys, u_tab, vt_tab, ln2_g, ln2_b = tail
    h, q = _merge(a, o, ga, gb, x, wa, wb, wo, ln1_g, ln1_b, wq, alpha)
    ra, rn, rb, rr1 = _peer_route(q, sub_keys)
    return _peer_experts(h, u_tab, vt_tab, ra, rn, rb, rr1, ln2_g, ln2_b, alpha)


def kernel(x_prompt, x_sample, cache_da_k, cache_da_v, cache_sa_k, cache_sa_v, cache_idx_k, page_table, w_in, lambda_q1, lambda_k1, lambda_q2, lambda_k2, da_subln_g, w_a_up, w_b_up, w_out, ln1_g, ln1_b, peer_wq, peer_sub_keys, peer_u, peer_v, ln2_g, ln2_b):
    depth = w_in.shape[0]
    bp, sp, d_model = x_prompt.shape
    bd, sd, _ = x_sample.shape
    assert sd == 1, "the sample pass handles one new token per sequence"
    n_pool, page = cache_da_k.shape[1], cache_da_k.shape[2]
    n_pages = page_table.shape[1]
    alpha = (2.0 * depth) ** 0.25
    row2 = lambda v: v.reshape(1, -1)

    da_kt = cache_da_k.transpose(0, 1, 3, 4, 5, 2).reshape(depth, n_pool, DA_WIDTH, page)
    sa_kt = cache_sa_k.transpose(0, 1, 3, 4, 2).reshape(depth, n_pool, SA_WIDTH, page)
    sa_vt = cache_sa_v.transpose(0, 1, 3, 4, 2).reshape(depth, n_pool, SA_WIDTH, page)
    idx_kt = cache_idx_k.transpose(0, 1, 3, 2)
    da_v = cache_da_v.reshape(depth, n_pool, page * DA_HEADS, 2 * DA_HD)

    y_p = x_prompt.reshape(bp * sp, d_model)
    y_s = x_sample.reshape(bd, d_model)
    p_new, s_new = [], []
    for l in range(depth):
        lam_init = 0.8 - 0.6 * math.exp(-0.3 * l)
        lv = jnp.stack([lambda_q1[l], lambda_k1[l], lambda_q2[l], lambda_k2[l]])
        g_sub = row2(da_subln_g[l])
        w_arr = _arrange_w_in(w_in[l], d_model)
        tail = (w_a_up[l].astype(BF16), w_b_up[l].astype(BF16), w_out[l].astype(BF16), row2(ln1_g[l]),
                row2(ln1_b[l]), peer_wq[l].astype(BF16), peer_sub_keys[l].astype(BF16),
                peer_u[l].astype(BF16), peer_v[l].T.astype(BF16), row2(ln2_g[l]), row2(ln2_b[l]))

        (qa, kaf, kab, vaf, vab, qs, ksf, ksb, vsf, vsb, qi, kif, ki2, wi, ga, gb) = _in_proj(y_p, w_arr)
        r3 = lambda v: v.reshape(bp, sp, v.shape[-1])
        a = _prompt_diff_attn(lv, g_sub, r3(qa), r3(kab), r3(vab), lam_init)
        o = _prompt_sparse_attn(r3(qi), r3(wi), r3(ki2), r3(qs), r3(ksb), r3(vsb))
        p_new.append((kaf.reshape(bp, sp, DA_HEADS, 2, DA_HD), vaf.reshape(bp, sp, DA_HEADS, 2 * DA_HD),
                      ksf.reshape(bp, sp, SA_HEADS, SA_HD), vsf.reshape(bp, sp, SA_HEADS, SA_HD),
                      kif.reshape(bp, sp, IDX_HD)))
        y_p = _layer_tail(y_p, a.reshape(bp * sp, DA_WIDTH), o.reshape(bp * sp, SA_WIDTH), ga, gb, tail, alpha)

        (qa, kaf, kab, vaf, vab, qs, ksf, ksb, vsf, vsb, qi, kif, ki2, wi, ga, gb) = _in_proj(y_s, w_arr)
        t3 = lambda v: v.reshape(bd, 1, v.shape[-1])
        scores = _sample_index_scores(page_table, qi.reshape(bd, IDX_HEADS, IDX_HD),
                                      wi[:, :IDX_HEADS].reshape(bd, IDX_HEADS, 1), t3(kif), idx_kt, l)
        k_sel = min(TOPK_MAX, (n_pages * page + 1) // 4)
        msk = _sample_select(scores, k_sel, page)
        a, o = _sample_attn(page_table, lv, g_sub, t3(qa), t3(qs), t3(kaf), t3(vaf), t3(ksf), t3(vsf),
                            msk.reshape(bd, 1, msk.shape[1]), da_kt, da_v, sa_kt, sa_vt, l, lam_init)
        s_new.append((kaf.reshape(bd, 1, DA_HEADS, 2, DA_HD), vaf.reshape(bd, 1, DA_HEADS, 2 * DA_HD),
                      ksf.reshape(bd, 1, SA_HEADS, SA_HD), vsf.reshape(bd, 1, SA_HEADS, SA_HD),
                      kif.reshape(bd, 1, IDX_HD)))
        y_s = _layer_tail(y_s, a.reshape(bd, DA_WIDTH), o.reshape(bd, SA_WIDTH), ga, gb, tail, alpha)

    stk = lambda lst, j: jnp.stack([e[j] for e in lst])
    return (y_p.reshape(bp, sp, d_model), y_s.reshape(bd, 1, d_model),
            stk(p_new, 0), stk(p_new, 1), stk(p_new, 2), stk(p_new, 3), stk(p_new, 4),
            stk(s_new, 0), stk(s_new, 1), stk(s_new, 2), stk(s_new, 3), stk(s_new, 4))
```

```python
import functools
import math

import jax
import jax.numpy as jnp
from jax import lax
from jax.experimental import pallas as pl
from jax.experimental.pallas import tpu as pltpu

F32 = jnp.float32
BF16 = jnp.bfloat16
NEG_INF = float("-inf")

LANES = 128
SUBLANES = 8
DA_HEADS = 4
DA_HD = 64
DA_WIDTH = DA_HEADS * 2 * DA_HD
SA_HEADS = 8
SA_HD = 64
SA_WIDTH = SA_HEADS * SA_HD
IDX_HEADS = 8
IDX_HD = 64
TOPK_MAX = 256
PEER_HEADS = 8
PEER_NKEYS = 128
PEER_QD = 256
PEER_TOPK = 16
LN_EPS = 1e-5
RMS_EPS = 1e-5
VMEM_LIMIT = 56 * 1024 * 1024


def _alibi_slopes(n):
    return [2.0 ** (-8.0 * (i + 1) / n) for i in range(n)]


def _dot(a, b):
    return jnp.dot(a, b, preferred_element_type=F32)


def _dot_nt(a, b):
    return lax.dot_general(a, b, (((1,), (1,)), ((), ())), preferred_element_type=F32)


def _fold_lanes(x, op):
    parts = [x[:, t * LANES:(t + 1) * LANES] for t in range(x.shape[1] // LANES)]
    return functools.reduce(op, parts)


def _params(sem):
    return pltpu.CompilerParams(dimension_semantics=sem, vmem_limit_bytes=VMEM_LIMIT)


def _layer_norm(x, g, b):
    mu = jnp.mean(x, axis=-1, keepdims=True)
    xc = x - mu
    var = jnp.mean(xc * xc, axis=-1, keepdims=True)
    return xc * lax.rsqrt(var + LN_EPS) * g + b


_SEG = dict(qa=0, ka=512, va=1024, qs=1536, ks=2048, vs=2560, qi=3072, ki=3584, wi=3712, ga=3840)


def _arrange_w_in(w_in, d_model):
    offs = [0]
    for n in (DA_WIDTH, DA_WIDTH, DA_WIDTH, SA_WIDTH, SA_WIDTH, SA_WIDTH,
              IDX_HEADS * IDX_HD, IDX_HD, IDX_HEADS, d_model, d_model):
        offs.append(offs[-1] + n)
    ki = w_in[:, offs[7]:offs[8]]
    wi = w_in[:, offs[8]:offs[9]]
    pad = jnp.zeros((w_in.shape[0], LANES - IDX_HEADS), w_in.dtype)
    return jnp.concatenate([w_in[:, :offs[7]], ki, ki, wi, pad, w_in[:, offs[9]:]], axis=1).astype(BF16)


def _in_proj_kernel(x_ref, w_ref, qa_o, kaf_o, kab_o, vaf_o, vab_o, qs_o, ksf_o, ksb_o, vsf_o, vsb_o,
                    qi_o, kif_o, ki2_o, wi_o, ga_o, gb_o, *, d_model, transposed):
    xb = x_ref[...].astype(BF16)
    cache = (lambda z: z.T) if transposed else (lambda z: z)

    def seg(name, n):
        a = _SEG[name]
        return _dot(xb, w_ref[:, a:a + n])

    qa_o[...] = (seg("qa", DA_WIDTH) * (DA_HD ** -0.5)).astype(BF16)
    z = seg("ka", DA_WIDTH)
    kaf_o[...] = cache(z)
    kab_o[...] = z.astype(BF16)
    z = seg("va", DA_WIDTH)
    vaf_o[...] = z
    vab_o[...] = z.astype(BF16)
    qs_o[...] = (seg("qs", SA_WIDTH) * (SA_HD ** -0.5)).astype(BF16)
    z = seg("ks", SA_WIDTH)
    ksf_o[...] = cache(z)
    ksb_o[...] = z.astype(BF16)
    z = seg("vs", SA_WIDTH)
    vsf_o[...] = cache(z)
    vsb_o[...] = z.astype(BF16)
    qi_o[...] = (seg("qi", IDX_HEADS * IDX_HD) * (IDX_HD ** -0.5)).astype(BF16)
    z = seg("ki", 2 * IDX_HD)
    kif_o[...] = z.T[:IDX_HD, :] if transposed else z[:, :IDX_HD]
    ki2_o[...] = z.astype(BF16)
    wi_o[...] = seg("wi", LANES) * (IDX_HEADS ** -0.5)
    ga_o[...] = seg("ga", d_model)
    gb_o[...] = _dot(xb, w_ref[:, _SEG["ga"] + d_model:_SEG["ga"] + 2 * d_model])


def _in_proj(x, w_arr, seq=None):
    t, d_model = x.shape
    tm = min(t, 256)
    assert t % tm == 0
    row = lambda n: pl.BlockSpec((tm, n), lambda i: (i, 0))
    widths = [(512, BF16), (512, F32), (512, BF16), (512, F32), (512, BF16), (512, BF16), (512, F32),
              (512, BF16), (512, F32), (512, BF16), (512, BF16), (IDX_HD, F32), (LANES, BF16), (LANES, F32),
              (d_model, F32), (d_model, F32)]
    shapes = [jax.ShapeDtypeStruct((t, n), dt) for n, dt in widths]
    specs = [row(n) for n, _ in widths]
    if seq is not None:
        assert seq % tm == 0
        per = seq // tm
        for j in (1, 6, 8, 11):
            n = widths[j][0]
            shapes[j] = jax.ShapeDtypeStruct((t // seq, n, seq), F32)
            specs[j] = pl.BlockSpec((None, n, tm), lambda i: (i // per, 0, i % per))
    return pl.pallas_call(
        functools.partial(_in_proj_kernel, d_model=d_model, transposed=seq is not None),
        out_shape=shapes,
        grid=(t // tm,),
        in_specs=[row(d_model), pl.BlockSpec(w_arr.shape, lambda i: (0, 0))],
        out_specs=specs,
        compiler_params=_params(("parallel",)),
        name="in_proj",
    )(x, w_arr)


_ORDERED_NEG_INF = -(2 ** 31) + 0x7FFFFF


def _float_of_ordered(k):
    return lax.bitcast_convert_type(k ^ ((k >> 31) & jnp.int32(0x7FFFFFFF)), F32)


def _topk_mask(key_scr, msk_scr, nk, k_sel, cw, idx_bits):
    rows = key_scr.shape[1]
    pos = lax.broadcasted_iota(jnp.int32, (rows, cw), 1)

    def count(pred_fn):
        def body(c, acc):
            hit = jnp.where(pred_fn(key_scr[c], pos + c * cw), 1.0, 0.0)
            return acc + _fold_lanes(hit, jnp.add)
        acc = lax.fori_loop(0, nk, body, jnp.zeros((rows, LANES), F32))
        return jnp.sum(acc, axis=1, keepdims=True)

    def thr_bit(it, off):
        cand = off + lax.shift_left(jnp.int32(1), 31 - it)
        t = _float_of_ordered(jnp.int32(_ORDERED_NEG_INF) + cand)
        cnt = count(lambda key, p: key >= t)
        return jnp.where(cnt >= k_sel, cand, off)

    off = lax.fori_loop(0, 32, thr_bit, jnp.zeros((rows, 1), jnp.int32))
    thr = _float_of_ordered(jnp.int32(_ORDERED_NEG_INF) + off)
    need = k_sel - count(lambda key, p: key > thr)

    def lim_bit(it, lim):
        cand = lim + lax.shift_left(jnp.int32(1), idx_bits - 1 - it)
        cnt = count(lambda key, p: (key == thr) & (p < cand))
        return jnp.where(cnt < need, cand, lim)

    tied = jnp.max(count(lambda key, p: key >= thr)) > k_sel
    lim = lax.cond(tied,
                   lambda: lax.fori_loop(0, idx_bits, lim_bit, jnp.zeros((rows, 1), jnp.int32)),
                   lambda: jnp.full((rows, 1), (1 << idx_bits) - 1, jnp.int32))

    def write(c, carry):
        key = key_scr[c]
        sel = (key > thr) | ((key == thr) & (pos + c * cw <= lim))
        msk_scr[c] = jnp.where(sel, 0.0, NEG_INF)
        return carry

    lax.fori_loop(0, nk, write, 0)


def _attend(units, q_of, kcols, vcols, slopes, k_ref, v_ref, msk_scr, s_scr, m_scr, l_scr, acc_scr,
            q0, nk, sq, ck):
    rel = (lax.broadcasted_iota(jnp.int32, (sq, ck), 0)
           - lax.broadcasted_iota(jnp.int32, (sq, ck), 1)).astype(F32)
    m_scr[...] = jnp.full(m_scr.shape, NEG_INF, F32)
    l_scr[...] = jnp.zeros(l_scr.shape, F32)
    acc_scr[...] = jnp.zeros(acc_scr.shape, F32)

    def pass1(kc, carry):
        k0 = pl.multiple_of(kc * ck, ck)
        dist = (q0 - k0).astype(F32) + rel
        ok = dist >= 0.0
        extra = None if msk_scr is None else msk_scr[kc]
        for u in units:
            kb = k_ref[pl.ds(k0, ck), kcols(u)]
            s = _dot_nt(q_of(u), kb) - slopes[u] * dist
            if extra is not None:
                s = s + extra
            s = jnp.where(ok, s, NEG_INF)
            s_scr[u, kc] = s
            m_scr[u] = jnp.maximum(m_scr[u], _fold_lanes(s, jnp.maximum))
        return carry

    lax.fori_loop(0, nk, pass1, 0)
    m_col = [jnp.max(m_scr[u], axis=1, keepdims=True) for u in units]

    def pass2(kc, carry):
        k0 = pl.multiple_of(kc * ck, ck)
        for u in units:
            p = jnp.exp(s_scr[u, kc] - m_col[u])
            l_scr[u] = l_scr[u] + _fold_lanes(p, jnp.add)
            acc_scr[u] = acc_scr[u] + _dot(p.astype(BF16), v_ref[pl.ds(k0, ck), vcols(u)])
        return carry

    lax.fori_loop(0, nk, pass2, 0)


def _half_masks(shape):
    lane = lax.broadcasted_iota(jnp.int32, shape, 1)
    return lane < (LANES // 2), lane >= (LANES // 2)


def _diff_lambda(lv, lam_init):
    d1 = jnp.sum(lv[0:1, :] * lv[1:2, :], axis=1, keepdims=True)
    d2 = jnp.sum(lv[2:3, :] * lv[3:4, :], axis=1, keepdims=True)
    return jnp.exp(d1) - jnp.exp(d2) + lam_init


def _diff_combine(o0, o1, lam, g, lam_init):
    o = o0 - lam * o1
    o = o * lax.rsqrt(jnp.mean(o * o, axis=-1, keepdims=True) + RMS_EPS)
    return o * g * (1.0 - lam_init)


def _diff_attn_kernel(lv_ref, g_ref, q_ref, k_ref, v_ref, o_ref, s_scr, m_scr, l_scr, acc_scr,
                      *, sq, ck, lam_init):
    j = pl.program_id(1)
    q0 = j * sq
    nk = (q0 + sq - 1) // ck + 1
    lo, hi = _half_masks((sq, LANES))
    qm = []
    for h in range(DA_HEADS):
        qh = q_ref[:, h * LANES:(h + 1) * LANES]
        qm.append(jnp.where(lo, qh, jnp.zeros_like(qh)))
        qm.append(jnp.where(hi, qh, jnp.zeros_like(qh)))
    units = list(range(2 * DA_HEADS))
    cols = lambda u: slice((u // 2) * LANES, (u // 2 + 1) * LANES)
    slopes = [s for s in _alibi_slopes(DA_HEADS) for _ in range(2)]
    _attend(units, lambda u: qm[u], cols, cols, slopes, k_ref, v_ref, None, s_scr, m_scr, l_scr, acc_scr,
            q0, nk, sq, ck)
    lam = _diff_lambda(lv_ref[...], lam_init)
    for h in range(DA_HEADS):
        o0 = acc_scr[2 * h] / jnp.sum(l_scr[2 * h], axis=1, keepdims=True)
        o1 = acc_scr[2 * h + 1] / jnp.sum(l_scr[2 * h + 1], axis=1, keepdims=True)
        o = _diff_combine(o0, o1, lam, g_ref[...], lam_init)
        o_ref[:, h * LANES:(h + 1) * LANES] = o.astype(o_ref.dtype)


def _prompt_diff_attn(lv, g, qa, ka, va, lam_init):
    b, s, _ = qa.shape
    sq = min(s, 128)
    ck = min(s, 512)
    nc = s // ck
    u = 2 * DA_HEADS
    blk_q = pl.BlockSpec((None, sq, DA_WIDTH), lambda bi, j: (bi, j, 0))
    blk_kv = pl.BlockSpec((None, s, DA_WIDTH), lambda bi, j: (bi, 0, 0))
    full = lambda a: pl.BlockSpec(a.shape, lambda bi, j: (0,) * a.ndim)
    return pl.pallas_call(
        functools.partial(_diff_attn_kernel, sq=sq, ck=ck, lam_init=lam_init),
        out_shape=jax.ShapeDtypeStruct((b, s, DA_WIDTH), BF16),
        grid=(b, s // sq),
        in_specs=[full(lv), full(g), blk_q, blk_kv, blk_kv],
        out_specs=blk_q,
        scratch_shapes=[pltpu.VMEM((u, nc, sq, ck), F32), pltpu.VMEM((u, sq, LANES), F32),
                        pltpu.VMEM((u, sq, LANES), F32), pltpu.VMEM((u, sq, LANES), F32)],
        compiler_params=_params(("parallel", "arbitrary")),
        name="prompt_diff_attn",
    )(lv, g, qa, ka, va)


def _index_scores_chunk(qim, wi, kblk):
    total = None
    for h in range(IDX_HEADS):
        r = jnp.maximum(_dot_nt(qim[h], kblk), 0.0)
        term = wi[:, h:h + 1] * r
        total = term if total is None else total + term
    return total


def _sparse_attn_kernel(qi_ref, wi_ref, ki_ref, qs_ref, ks_ref, vs_ref, o_ref,
                        key_scr, msk_scr, s_scr, m_scr, l_scr, acc_scr, *, sq, ck, k_sel, idx_bits):
    j = pl.program_id(1)
    q0 = j * sq
    nk = (q0 + sq - 1) // ck + 1
    lo, hi = _half_masks((sq, LANES))

    def masked(ref):
        out = []
        for p in range(ref.shape[1] // LANES):
            blk = ref[:, p * LANES:(p + 1) * LANES]
            out.append(jnp.where(lo, blk, jnp.zeros_like(blk)))
            out.append(jnp.where(hi, blk, jnp.zeros_like(blk)))
        return out

    qim = masked(qi_ref)
    wi = wi_ref[...]
    rel = (lax.broadcasted_iota(jnp.int32, (sq, ck), 0) - lax.broadcasted_iota(jnp.int32, (sq, ck), 1))

    def scores(kc, carry):
        k0 = pl.multiple_of(kc * ck, ck)
        tot = _index_scores_chunk(qim, wi, ki_ref[pl.ds(k0, ck), :])
        valid = (q0 - k0) + rel >= 0
        key_scr[kc] = jnp.where(valid, tot, NEG_INF)
        return carry

    lax.fori_loop(0, nk, scores, 0)
    _topk_mask(key_scr, msk_scr, nk, k_sel, ck, idx_bits)

    qsm = masked(qs_ref)
    units = list(range(SA_HEADS))
    cols = lambda u: slice((u // 2) * LANES, (u // 2 + 1) * LANES)
    _attend(units, lambda u: qsm[u], cols, cols, _alibi_slopes(SA_HEADS), ks_ref, vs_ref, msk_scr,
            s_scr, m_scr, l_scr, acc_scr, q0, nk, sq, ck)
    for p in range(SA_HEADS // 2):
        oa = acc_scr[2 * p] / jnp.sum(l_scr[2 * p], axis=1, keepdims=True)
        ob = acc_scr[2 * p + 1] / jnp.sum(l_scr[2 * p + 1], axis=1, keepdims=True)
        o_ref[:, p * LANES:(p + 1) * LANES] = jnp.where(lo, oa, ob).astype(o_ref.dtype)


def _prompt_sparse_attn(qi, wi, ki2, qs, ks, vs):
    b, s, _ = qs.shape
    sq = min(s, 256)
    ck = min(s, 512)
    nc = s // ck
    k_sel = min(TOPK_MAX, s // 4)
    idx_bits = max(1, (s - 1).bit_length())
    blk_q = lambda w: pl.BlockSpec((None, sq, w), lambda bi, j: (bi, j, 0))
    blk_kv = lambda w: pl.BlockSpec((None, s, w), lambda bi, j: (bi, 0, 0))
    u = SA_HEADS
    return pl.pallas_call(
        functools.partial(_sparse_attn_kernel, sq=sq, ck=ck, k_sel=k_sel, idx_bits=idx_bits),
        out_shape=jax.ShapeDtypeStruct((b, s, SA_WIDTH), BF16),
        grid=(b, s // sq),
        in_specs=[blk_q(IDX_HEADS * IDX_HD), blk_q(LANES), blk_kv(LANES),
                  blk_q(SA_WIDTH), blk_kv(SA_WIDTH), blk_kv(SA_WIDTH)],
        out_specs=blk_q(SA_WIDTH),
        scratch_shapes=[pltpu.VMEM((nc, sq, ck), F32), pltpu.VMEM((nc, sq, ck), F32),
                        pltpu.VMEM((u, nc, sq, ck), F32), pltpu.VMEM((u, sq, LANES), F32),
                        pltpu.VMEM((u, sq, LANES), F32), pltpu.VMEM((u, sq, LANES), F32)],
        compiler_params=_params(("parallel", "arbitrary")),
        name="prompt_sparse_attn",
    )(qi, wi, ki2, qs, ks, vs)


def _sample_index_kernel(pt_ref, qi_ref, wi_ref, kin_ref, cache_ref, o_ref, buf, sem,
                         *, layer, sb, n_pages, page):
    g = pl.program_id(0)
    ng = pl.num_programs(0)
    past = n_pages * page

    def copies(step, slot):
        out = []
        for s in range(sb):
            for p in range(n_pages):
                src = cache_ref.at[layer, pt_ref[step * sb + s, p]]
                out.append(pltpu.make_async_copy(src, buf.at[slot, s, :, pl.ds(p * page, page)], sem.at[slot]))
        return out

    slot = g % 2

    @pl.when(g == 0)
    def _():
        for c in copies(0, 0):
            c.start()

    @pl.when(g + 1 < ng)
    def _():
        for c in copies(g + 1, 1 - slot):
            c.start()

    for c in copies(g, slot):
        c.wait()

    lane = lax.broadcasted_iota(jnp.int32, (1, page), 1)
    for s in range(sb):
        q = qi_ref[s]
        w = wi_ref[s]
        r = jnp.maximum(_dot(q, buf[slot, s].astype(BF16)), 0.0)
        o_ref[s:s + 1, 0:past] = jnp.sum(w * r, axis=0, keepdims=True)
        kn = kin_ref[s].astype(BF16).astype(F32)
        rn = jnp.maximum(jnp.sum(q.astype(F32) * kn, axis=1, keepdims=True), 0.0)
        tot = jnp.sum(w * rn, axis=0, keepdims=True)
        o_ref[s:s + 1, past:past + page] = jnp.where(lane == 0, tot, NEG_INF)


def _sample_index_scores(page_table, qi8, wi8, ki_new, idx_kt, layer):
    bd, n_pages = page_table.shape
    page = idx_kt.shape[3]
    sb = min(bd, SUBLANES)
    assert bd % sb == 0
    lpad = (n_pages + 1) * page
    grid_spec = pltpu.PrefetchScalarGridSpec(
        num_scalar_prefetch=1,
        grid=(bd // sb,),
        in_specs=[
            pl.BlockSpec((sb, IDX_HEADS, IDX_HD), lambda g, pt: (g, 0, 0)),
            pl.BlockSpec((sb, IDX_HEADS, 1), lambda g, pt: (g, 0, 0)),
            pl.BlockSpec((sb, 1, IDX_HD), lambda g, pt: (g, 0, 0)),
            pl.BlockSpec(memory_space=pl.ANY),
        ],
        out_specs=pl.BlockSpec((sb, lpad), lambda g, pt: (g, 0)),
        scratch_shapes=[pltpu.VMEM((2, sb, IDX_HD, n_pages * page), F32), pltpu.SemaphoreType.DMA((2,))],
    )
    return pl.pallas_call(
        functools.partial(_sample_index_kernel, layer=layer, sb=sb, n_pages=n_pages, page=page),
        out_shape=jax.ShapeDtypeStruct((bd, lpad), F32),
        grid_spec=grid_spec,
        compiler_params=_params(("arbitrary",)),
        name="sample_index_scores",
    )(page_table, qi8, wi8, ki_new, idx_kt)


def _sample_select_kernel(sc_ref, msk_ref, key_scr, msk_scr, *, k_sel, cw, idx_bits):
    nc = key_scr.shape[0]
    for c in range(nc):
        key_scr[c] = sc_ref[:, c * cw:(c + 1) * cw]
    _topk_mask(key_scr, msk_scr, nc, k_sel, cw, idx_bits)
    for c in range(nc):
        msk_ref[:, c * cw:(c + 1) * cw] = msk_scr[c]


def _sample_select(scores, k_sel, cw):
    bd, lpad = scores.shape
    nc = lpad // cw
    idx_bits = max(1, (lpad - 1).bit_length())
    return pl.pallas_call(
        functools.partial(_sample_select_kernel, k_sel=k_sel, cw=cw, idx_bits=idx_bits),
        out_shape=jax.ShapeDtypeStruct((bd, lpad), F32),
        grid=(1,),
        in_specs=[pl.BlockSpec((bd, lpad), lambda i: (0, 0))],
        out_specs=pl.BlockSpec((bd, lpad), lambda i: (0, 0)),
        scratch_shapes=[pltpu.VMEM((nc, bd, cw), F32), pltpu.VMEM((nc, bd, cw), F32)],
        compiler_params=_params(("arbitrary",)),
        name="sample_select",
    )(scores)


def _online_update(s, pv_fn, m_scr, l_scr, acc_scr):
    m_old = m_scr[...]
    m_new = jnp.maximum(m_old, jnp.max(s, axis=1, keepdims=True))
    m_safe = jnp.where(m_new == NEG_INF, 0.0, m_new)
    alpha = jnp.exp(m_old - m_safe)
    p = jnp.exp(s - m_safe)
    l_scr[...] = alpha * l_scr[...] + jnp.sum(p, axis=1, keepdims=True)
    acc_scr[...] = alpha * acc_scr[...] + pv_fn(p.astype(BF16))
    m_scr[...] = m_new


def _sample_attn_kernel(pt_ref, lv_ref, g_ref, qa_ref, qs_ref, kan_ref, van_ref, ksn_ref, vsn_ref,
                        msk_ref, mskn_ref, dak_hbm, dav_hbm, sak_hbm, sav_hbm, a_ref, o_ref,
                        dak_buf, dav_buf, sak_buf, sav_buf, sem, qda_scr, qsa_scr, m_scr, l_scr, acc_scr,
                        *, layer, n_pages, pg, page, lam_init):
    b = pl.program_id(0)
    hf = pl.program_id(1)
    nh = pl.num_programs(1)
    step = b * nh + hf
    n_steps = pl.num_programs(0) * nh
    slot = step % 2
    past = n_pages * page
    span = pg * page

    def copies(bb, hh, sl):
        out = []
        for j in range(pg):
            pid = pt_ref[bb, hh * pg + j]
            cols = pl.ds(j * page, page)
            out.append(pltpu.make_async_copy(dak_hbm.at[layer, pid], dak_buf.at[sl, :, cols], sem.at[sl, 0]))
            out.append(pltpu.make_async_copy(dav_hbm.at[layer, pid],
                                             dav_buf.at[sl, pl.ds(j * page * DA_HEADS, page * DA_HEADS), :],
                                             sem.at[sl, 1]))
            out.append(pltpu.make_async_copy(sak_hbm.at[layer, pid], sak_buf.at[sl, :, cols], sem.at[sl, 2]))
            out.append(pltpu.make_async_copy(sav_hbm.at[layer, pid], sav_buf.at[sl, :, cols], sem.at[sl, 3]))
        return out

    @pl.when(step == 0)
    def _():
        for c in copies(0, 0, 0):
            c.start()

    @pl.when(step + 1 < n_steps)
    def _():
        wrap = hf + 1 == nh
        for c in copies(jnp.where(wrap, b + 1, b), jnp.where(wrap, 0, hf + 1), 1 - slot):
            c.start()

    row = lax.broadcasted_iota(jnp.int32, (SUBLANES, DA_WIDTH), 0)
    col = lax.broadcasted_iota(jnp.int32, (SUBLANES, DA_WIDTH), 1)
    row1 = lax.broadcasted_iota(jnp.int32, (SUBLANES, 1), 0)
    slope_da = jnp.zeros((SUBLANES, 1), F32)
    for h, sl in enumerate(_alibi_slopes(DA_HEADS)):
        slope_da = jnp.where(row1 // 2 == h, sl, slope_da)
    slope_sa = jnp.zeros((SUBLANES, 1), F32)
    for h, sl in enumerate(_alibi_slopes(SA_HEADS)):
        slope_sa = jnp.where(row1 == h, sl, slope_sa)

    @pl.when(hf == 0)
    def _():
        qa = jnp.broadcast_to(qa_ref[...].astype(F32), (SUBLANES, DA_WIDTH))
        qda_scr[...] = jnp.where(col // DA_HD == row, qa, 0.0)
        qs = jnp.broadcast_to(qs_ref[...].astype(F32), (SUBLANES, SA_WIDTH))
        qsa_scr[...] = jnp.where(col // SA_HD == row, qs, 0.0)
        m_scr[...] = jnp.full(m_scr.shape, NEG_INF, F32)
        l_scr[...] = jnp.zeros(l_scr.shape, F32)
        acc_scr[...] = jnp.zeros(acc_scr.shape, F32)

    for c in copies(b, hf, slot):
        c.wait()

    kpos = hf * span + lax.broadcasted_iota(jnp.int32, (1, span), 1)
    dist = (past - kpos).astype(F32)

    def pv_da(p):
        return jnp.concatenate(
            [_dot(p, dav_buf[slot, pl.ds(h, span, stride=DA_HEADS), :].astype(BF16)) for h in range(DA_HEADS)],
            axis=1)

    s = _dot(qda_scr[...].astype(BF16), dak_buf[slot].astype(BF16)) - slope_da * dist
    _online_update(s, pv_da, m_scr.at[0], l_scr.at[0], acc_scr.at[0])
    s = _dot(qsa_scr[...].astype(BF16), sak_buf[slot].astype(BF16)) - slope_sa * dist + msk_ref[...]
    _online_update(s, lambda p: _dot_nt(p, sav_buf[slot].astype(BF16)), m_scr.at[1], l_scr.at[1], acc_scr.at[1])

    @pl.when(hf == nh - 1)
    def _():
        rnd = lambda ref: ref[...].astype(BF16).astype(F32)
        s = jnp.sum(qda_scr[...] * rnd(kan_ref), axis=1, keepdims=True)
        _online_update(s, lambda p: p.astype(F32) * rnd(van_ref), m_scr.at[0], l_scr.at[0], acc_scr.at[0])
        s = jnp.sum(qsa_scr[...] * rnd(ksn_ref), axis=1, keepdims=True) + mskn_ref[:, 0:1]
        _online_update(s, lambda p: p.astype(F32) * rnd(vsn_ref), m_scr.at[1], l_scr.at[1], acc_scr.at[1])

        lam = _diff_lambda(lv_ref[...], lam_init)
        oda = acc_scr[0] / l_scr[0]
        for h in range(DA_HEADS):
            cs = slice(h * LANES, (h + 1) * LANES)
            o = _diff_combine(oda[2 * h:2 * h + 1, cs], oda[2 * h + 1:2 * h + 2, cs], lam, g_ref[...], lam_init)
            a_ref[:, cs] = o.astype(a_ref.dtype)
        osa = acc_scr[1] / l_scr[1]
        lo, _ = _half_masks((1, LANES))
        for pr in range(SA_HEADS // 2):
            cs = slice(pr * LANES, (pr + 1) * LANES)
            o_ref[:, cs] = jnp.where(lo, osa[2 * pr:2 * pr + 1, cs], osa[2 * pr + 1:2 * pr + 2, cs]).astype(o_ref.dtype)


def _sample_attn(page_table, lv, g, qa, qs, ka_new, va_new, ks_new, vs_new, msk, da_kt, da_v, sa_kt, sa_vt,
                 layer, lam_init):
    bd, n_pages = page_table.shape
    page = da_kt.shape[3]
    pg = math.gcd(n_pages, 8)
    nh = n_pages // pg
    span = pg * page
    tok = lambda w: pl.BlockSpec((None, 1, w), lambda b, hf, pt: (b, 0, 0))
    full = lambda a: pl.BlockSpec(a.shape, lambda b, hf, pt: (0,) * a.ndim)
    hbm = pl.BlockSpec(memory_space=pl.ANY)
    grid_spec = pltpu.PrefetchScalarGridSpec(
        num_scalar_prefetch=1,
        grid=(bd, nh),
        in_specs=[full(lv), full(g), tok(DA_WIDTH), tok(SA_WIDTH), tok(DA_WIDTH), tok(DA_WIDTH),
                  tok(SA_WIDTH), tok(SA_WIDTH),
                  pl.BlockSpec((None, 1, span), lambda b, hf, pt: (b, 0, hf)),
                  pl.BlockSpec((None, 1, page), lambda b, hf, pt: (b, 0, n_pages)),
                  hbm, hbm, hbm, hbm],
        out_specs=[tok(DA_WIDTH), tok(SA_WIDTH)],
        scratch_shapes=[pltpu.VMEM((2, DA_WIDTH, span), F32), pltpu.VMEM((2, span * DA_HEADS, LANES), F32),
                        pltpu.VMEM((2, SA_WIDTH, span), F32), pltpu.VMEM((2, SA_WIDTH, span), F32),
                        pltpu.SemaphoreType.DMA((2, 4)),
                        pltpu.VMEM((SUBLANES, DA_WIDTH), F32), pltpu.VMEM((SUBLANES, SA_WIDTH), F32),
                        pltpu.VMEM((2, SUBLANES, 1), F32), pltpu.VMEM((2, SUBLANES, 1), F32),
                        pltpu.VMEM((2, SUBLANES, DA_WIDTH), F32)],
    )
    return pl.pallas_call(
        functools.partial(_sample_attn_kernel, layer=layer, n_pages=n_pages, pg=pg, page=page, lam_init=lam_init),
        out_shape=[jax.ShapeDtypeStruct((bd, 1, DA_WIDTH), BF16), jax.ShapeDtypeStruct((bd, 1, SA_WIDTH), BF16)],
        grid_spec=grid_spec,
        compiler_params=_params(("arbitrary", "arbitrary")),
        name="sample_attn",
    )(page_table, lv, g, qa, qs, ka_new, va_new, ks_new, vs_new, msk, msk, da_kt, da_v, sa_kt, sa_vt)


def _merge_kernel(a_ref, o_ref, ga_ref, gb_ref, x_ref, wa_ref, wb_ref, wo_ref, g_ref, b_ref, wq_ref,
                  h_ref, q_ref, *, alpha):
    merged = (jax.nn.sigmoid(ga_ref[...]) * _dot(a_ref[...], wa_ref[...])
              + jax.nn.sigmoid(gb_ref[...]) * _dot(o_ref[...], wb_ref[...]))
    h = _layer_norm(alpha * x_ref[...] + _dot(merged.astype(BF16), wo_ref[...]), g_ref[...], b_ref[...])
    h_ref[...] = h
    q = _dot(h.astype(BF16), wq_ref[...]).astype(q_ref.dtype)
    for hd in range(PEER_HEADS):
        q_ref[hd] = q[:, hd * PEER_QD:(hd + 1) * PEER_QD]


def _merge(a, o, ga, gb, x, wa, wb, wo, g, b, wq, alpha):
    t, d_model = x.shape
    tm = min(t, 256)
    row = lambda n: pl.BlockSpec((tm, n), lambda i: (i, 0))
    full = lambda arr: pl.BlockSpec(arr.shape, lambda i: (0,) * arr.ndim)
    return pl.pallas_call(
        functools.partial(_merge_kernel, alpha=alpha),
        out_shape=[jax.ShapeDtypeStruct((t, d_model), F32), jax.ShapeDtypeStruct((PEER_HEADS, t, PEER_QD), BF16)],
        grid=(t // tm,),
        in_specs=[row(DA_WIDTH), row(SA_WIDTH), row(d_model), row(d_model), row(d_model),
                  full(wa), full(wb), full(wo), full(g), full(b), full(wq)],
        out_specs=[row(d_model), pl.BlockSpec((PEER_HEADS, tm, PEER_QD), lambda i: (0, i, 0))],
        compiler_params=_params(("parallel",)),
        name="merge_ln_peerq",
    )(a, o, ga, gb, x, wa, wb, wo, g, b, wq)


def _take_max(s, ridx, exact):
    m = jnp.max(s, axis=0, keepdims=True)
    if not exact:
        return m, s == m
    first = jnp.min(jnp.where(s == m, ridx, float(s.shape[0])), axis=0, keepdims=True)
    return m, ridx == first


def _count_mismatch(flags, want):
    return jnp.max(jnp.abs(jnp.sum(flags, axis=0, keepdims=True) - float(want))) > 0.0


_CAND_LIM = [PEER_TOPK // (a + 1) for a in range(SUBLANES)]


def _top_subkeys(s0, s1, r0_scr, r1_scr, v_scr, exact):
    topk = PEER_TOPK
    rows, t = s0.shape
    ridx = lax.broadcasted_iota(jnp.int32, (rows, t), 0).astype(F32)
    kidx = lax.broadcasted_iota(jnp.int32, (topk, t), 0)
    r0_scr[...] = jnp.full((rows, t), float(topk), F32)
    r1_scr[...] = jnp.full((rows, t), float(topk), F32)

    def body(k, carry):
        c0, c1, v0, v1 = carry
        kf = jnp.asarray(k).astype(F32)
        m0, hit0 = _take_max(c0, ridx, exact)
        m1, hit1 = _take_max(c1, ridx, exact)
        r0_scr[...] = jnp.where(hit0, kf, r0_scr[...])
        r1_scr[...] = jnp.where(hit1, kf, r1_scr[...])
        return (jnp.where(hit0, NEG_INF, c0), jnp.where(hit1, NEG_INF, c1),
                jnp.where(kidx == k, m0, v0), jnp.where(kidx == k, m1, v1))

    zero = jnp.zeros((topk, t), F32)
    _, _, v0, v1 = lax.fori_loop(0, topk, body, (s0, s1, zero, zero))
    v_scr[0] = v0
    v_scr[1] = v1


def _top_candidates(cand, taken_scr, tv_scr, exact):
    topk = PEER_TOPK
    crows, t = cand.shape
    cidx = lax.broadcasted_iota(jnp.int32, (crows, t), 0).astype(F32)
    kidx = lax.broadcasted_iota(jnp.int32, (topk, t), 0)

    def body(k, carry):
        c, taken, tv = carry
        m, hit = _take_max(c, cidx, exact)
        return jnp.where(hit, NEG_INF, c), jnp.where(hit, 1.0, taken), jnp.where(kidx == k, m, tv)

    _, taken, tv = lax.fori_loop(0, topk, body, (cand, jnp.zeros((crows, t), F32), jnp.zeros((topk, t), F32)))
    taken_scr[...] = taken
    tv_scr[...] = tv


def _peer_route_head(q0, q1, sk0, sk1, r0_scr, r1_scr, v_scr, taken_scr, tv_scr):
    topk = PEER_TOPK
    s0 = _dot_nt(sk0, q0)
    s1 = _dot_nt(sk1, q1)
    rows, t = s0.shape

    _top_subkeys(s0, s1, r0_scr, r1_scr, v_scr, exact=False)
    in_top = lambda r_scr: jnp.where(r_scr[...] < float(topk), 1.0, 0.0)

    @pl.when(_count_mismatch(in_top(r0_scr), topk) | _count_mismatch(in_top(r1_scr), topk))
    def _():
        _top_subkeys(s0, s1, r0_scr, r1_scr, v_scr, exact=True)

    v0 = v_scr[0]
    v1 = v_scr[1]
    sub = lax.broadcasted_iota(jnp.int32, (SUBLANES, t), 0)
    groups = [v0[0:1, :] + v1]
    for a in range(1, SUBLANES):
        groups.append(jnp.where(sub < _CAND_LIM[a], v0[a:a + 1, :] + v1[0:SUBLANES, :], NEG_INF))
    groups.append(v0[SUBLANES:topk, :] + v1[0:1, :])
    cand = jnp.concatenate(groups, axis=0)
    _top_candidates(cand, taken_scr, tv_scr, exact=False)

    @pl.when(_count_mismatch(taken_scr[...], topk))
    def _():
        _top_candidates(cand, taken_scr, tv_scr, exact=True)

    taken = taken_scr[...]
    tv = tv_scr[...]
    z = jnp.sum(jnp.exp(tv - tv[0:1, :]), axis=0, keepdims=True)

    r0 = r0_scr[...]
    r1 = r1_scr[...]
    n_sel = jnp.zeros((rows, t), F32)
    for a in range(topk):
        if a == 0:
            n_a = jnp.sum(taken[0:topk, :], axis=0, keepdims=True)
        elif a < SUBLANES:
            base = topk + SUBLANES * (a - 1)
            n_a = jnp.sum(taken[base:base + SUBLANES, :], axis=0, keepdims=True)
        else:
            base = topk + SUBLANES * (SUBLANES - 1) + (a - SUBLANES)
            n_a = taken[base:base + 1, :]
        n_sel = jnp.where(r0 == float(a), n_a, n_sel)
    a_out = jnp.where(r0 < float(topk), jnp.exp(s0 - v0[0:1, :]) / z, 0.0)
    b_out = jnp.where(r1 < float(topk), jnp.exp(s1 - v1[0:1, :]), 0.0)
    return a_out, n_sel, b_out, r1


def _peer_route_kernel(q_ref, sk_ref, a_ref, n_ref, b_ref, r1_ref, r0_scr, r1_scr, v_scr, taken_scr, tv_scr):
    half = PEER_QD // 2

    def head(h, carry):
        a, n, b, r1 = _peer_route_head(q_ref[h, :, :half], q_ref[h, :, half:], sk_ref[h, 0], sk_ref[h, 1],
                                       r0_scr, r1_scr, v_scr, taken_scr, tv_scr)
        a_ref[h] = a
        n_ref[h] = n
        b_ref[h] = b.astype(b_ref.dtype)
        r1_ref[h] = r1.astype(r1_ref.dtype)
        return carry

    lax.fori_loop(0, PEER_HEADS, head, 0)


def _peer_route(q, sub_keys):
    t = q.shape[1]
    tt = min(t, LANES)
    blk = pl.BlockSpec((PEER_HEADS, PEER_NKEYS, tt), lambda i: (0, 0, i))
    shp = lambda dt: jax.ShapeDtypeStruct((PEER_HEADS, PEER_NKEYS, t), dt)
    return pl.pallas_call(
        _peer_route_kernel,
        out_shape=[shp(F32), shp(F32), shp(BF16), shp(BF16)],
        grid=(t // tt,),
        in_specs=[pl.BlockSpec((PEER_HEADS, tt, PEER_QD), lambda i: (0, i, 0)),
                  pl.BlockSpec(sub_keys.shape, lambda i: (0, 0, 0, 0))],
        out_specs=[blk] * 4,
        scratch_shapes=[pltpu.VMEM((PEER_NKEYS, tt), F32), pltpu.VMEM((PEER_NKEYS, tt), F32),
                        pltpu.VMEM((2, PEER_TOPK, tt), F32),
                        pltpu.VMEM((PEER_TOPK + SUBLANES * SUBLANES, tt), F32), pltpu.VMEM((PEER_TOPK, tt), F32)],
        compiler_params=_params(("parallel",)),
        name="peer_route",
    )(q, sub_keys)


def _gelu(x):
    return 0.5 * x * (1.0 + lax.erf(x * (2.0 ** -0.5)))


def _peer_expert_kernel(h_ref, u_ref, vt_ref, a_ref, n_ref, b_ref, r1_ref, g_ref, beta_ref, y_ref,
                        hb_scr, pre_scr, act_scr, acc_scr, *, ci, alpha):
    c = pl.program_id(1)

    @pl.when(c == 0)
    def _():
        hb_scr[...] = h_ref[...].astype(BF16)
        acc_scr[...] = jnp.zeros(acc_scr.shape, F32)

    pre_scr[...] = _dot_nt(u_ref[...], hb_scr[...])
    for i in range(ci):
        rows = slice(i * PEER_NKEYS, (i + 1) * PEER_NKEYS)
        w = None
        for hd in range(PEER_HEADS):
            a_row = a_ref[hd, i:i + 1, :].astype(BF16)
            n_row = n_ref[hd, i:i + 1, :].astype(BF16)
            b = b_ref[hd]
            gate = a_row * jnp.where(r1_ref[hd] < n_row, b, jnp.zeros_like(b))
            w = gate if w is None else w + gate
        act_scr[rows, :] = _gelu(pre_scr[rows, :]).astype(BF16) * w
    acc_scr[...] = acc_scr[...] + _dot(vt_ref[...], act_scr[...])

    @pl.when(c == pl.num_programs(1) - 1)
    def _():
        hv = h_ref[...]
        y_ref[...] = _layer_norm(alpha * hv + acc_scr[...].T, g_ref[...], beta_ref[...])


def _peer_experts(h, u_tab, vt_tab, ra, rn, rb, rr1, g, beta, alpha):
    t, d_model = h.shape
    tt = min(t, 512)
    ci = SUBLANES
    ne = ci * PEER_NKEYS
    nch = u_tab.shape[0] // ne
    sml = pl.BlockSpec((PEER_HEADS, ci, tt), lambda i, c: (0, c, i))
    big = pl.BlockSpec((PEER_HEADS, PEER_NKEYS, tt), lambda i, c: (0, 0, i))
    full = lambda arr: pl.BlockSpec(arr.shape, lambda i, c: (0,) * arr.ndim)
    return pl.pallas_call(
        functools.partial(_peer_expert_kernel, ci=ci, alpha=alpha),
        out_shape=jax.ShapeDtypeStruct((t, d_model), F32),
        grid=(t // tt, nch),
        in_specs=[pl.BlockSpec((tt, d_model), lambda i, c: (i, 0)),
                  pl.BlockSpec((ne, d_model), lambda i, c: (c, 0)),
                  pl.BlockSpec((d_model, ne), lambda i, c: (0, c)),
                  sml, sml, big, big, full(g), full(beta)],
        out_specs=pl.BlockSpec((tt, d_model), lambda i, c: (i, 0)),
        scratch_shapes=[pltpu.VMEM((tt, d_model), BF16), pltpu.VMEM((ne, tt), F32),
                        pltpu.VMEM((ne, tt), BF16), pltpu.VMEM((d_model, tt), F32)],
        compiler_params=_params(("parallel", "arbitrary")),
        name="peer_experts",
    )(h, u_tab, vt_tab, ra, rn, rb, rr1, g, beta)


def _layer_tail(x, a, o, ga, gb, tail, alpha):
    wa, wb, wo, ln1_g, ln1_b, wq, sub_keys, u_tab, vt_tab, ln2_g, ln2_b = tail
    h, q = _merge(a, o, ga, gb, x, wa, wb, wo, ln1_g, ln1_b, wq, alpha)
    ra, rn, rb, rr1 = _peer_route(q, sub_keys)
    return _peer_experts(h, u_tab, vt_tab, ra, rn, rb, rr1, ln2_g, ln2_b, alpha)


def kernel(x_prompt, x_sample, cache_da_k, cache_da_v, cache_sa_k, cache_sa_v, cache_idx_k, page_table, w_in, lambda_q1, lambda_k1, lambda_q2, lambda_k2, da_subln_g, w_a_up, w_b_up, w_out, ln1_g, ln1_b, peer_wq, peer_sub_keys, peer_u, peer_v, ln2_g, ln2_b):
    depth = w_in.shape[0]
    bp, sp, d_model = x_prompt.shape
    bd, sd, _ = x_sample.shape
    assert sd == 1, "the sample pass handles one new token per sequence"
    n_pool, page = cache_da_k.shape[1], cache_da_k.shape[2]
    n_pages = page_table.shape[1]
    alpha = (2.0 * depth) ** 0.25
    row2 = lambda v: v.reshape(1, -1)

    da_kt = cache_da_k.transpose(0, 1, 3, 4, 5, 2).reshape(depth, n_pool, DA_WIDTH, page)
    sa_kt = cache_sa_k.transpose(0, 1, 3, 4, 2).reshape(depth, n_pool, SA_WIDTH, page)
    sa_vt = cache_sa_v.transpose(0, 1, 3, 4, 2).reshape(depth, n_pool, SA_WIDTH, page)
    idx_kt = cache_idx_k.transpose(0, 1, 3, 2)
    da_v = cache_da_v.reshape(depth, n_pool, page * DA_HEADS, 2 * DA_HD)

    y_p = x_prompt.reshape(bp * sp, d_model)
    y_s = x_sample.reshape(bd, d_model)
    p_new, s_new = [], []
    for l in range(depth):
        lam_init = 0.8 - 0.6 * math.exp(-0.3 * l)
        lv = jnp.stack([lambda_q1[l], lambda_k1[l], lambda_q2[l], lambda_k2[l]])
        g_sub = row2(da_subln_g[l])
        w_arr = _arrange_w_in(w_in[l], d_model)
        tail = (w_a_up[l].astype(BF16), w_b_up[l].astype(BF16), w_out[l].astype(BF16), row2(ln1_g[l]),
                row2(ln1_b[l]), peer_wq[l].astype(BF16), peer_sub_keys[l].astype(BF16),
                peer_u[l].astype(BF16), peer_v[l].T.astype(BF16), row2(ln2_g[l]), row2(ln2_b[l]))

        (qa, kat, kab, vaf, vab, qs, kst, ksb, vst, vsb, qi, kit, ki2, wi, ga, gb) = _in_proj(y_p, w_arr, seq=sp)
        r3 = lambda v: v.reshape(bp, sp, v.shape[-1])
        a = _prompt_diff_attn(lv, g_sub, r3(qa), r3(kab), r3(vab), lam_init)
        o = _prompt_sparse_attn(r3(qi), r3(wi), r3(ki2), r3(qs), r3(ksb), r3(vsb))
        p_new.append((kat.reshape(bp, DA_HEADS, 2, DA_HD, sp).transpose(0, 4, 1, 2, 3),
                      vaf.reshape(bp, sp, DA_HEADS, 2 * DA_HD),
                      kst.reshape(bp, SA_HEADS, SA_HD, sp).transpose(0, 3, 1, 2),
                      vst.reshape(bp, SA_HEADS, SA_HD, sp).transpose(0, 3, 1, 2),
                      kit.transpose(0, 2, 1)))
        y_p = _layer_tail(y_p, a.reshape(bp * sp, DA_WIDTH), o.reshape(bp * sp, SA_WIDTH), ga, gb, tail, alpha)

        (qa, kaf, kab, vaf, vab, qs, ksf, ksb, vsf, vsb, qi, kif, ki2, wi, ga, gb) = _in_proj(y_s, w_arr)
        t3 = lambda v: v.reshape(bd, 1, v.shape[-1])
        scores = _sample_index_scores(page_table, qi.reshape(bd, IDX_HEADS, IDX_HD),
                                      wi[:, :IDX_HEADS].reshape(bd, IDX_HEADS, 1), t3(kif), idx_kt, l)
        k_sel = min(TOPK_MAX, (n_pages * page + 1) // 4)
        msk = _sample_select(scores, k_sel, page)
        a, o = _sample_attn(page_table, lv, g_sub, t3(qa), t3(qs), t3(kaf), t3(vaf), t3(ksf), t3(vsf),
                            msk.reshape(bd, 1, msk.shape[1]), da_kt, da_v, sa_kt, sa_vt, l, lam_init)
        s_new.append((kaf.reshape(bd, 1, DA_HEADS, 2, DA_HD), vaf.reshape(bd, 1, DA_HEADS, 2 * DA_HD),
                      ksf.reshape(bd, 1, SA_HEADS, SA_HD), vsf.reshape(bd, 1, SA_HEADS, SA_HD),
                      kif.reshape(bd, 1, IDX_HD)))
        y_s = _layer_tail(y_s, a.reshape(bd, DA_WIDTH), o.reshape(bd, SA_WIDTH), ga, gb, tail, alpha)

    stk = lambda lst, j: jnp.stack([e[j] for e in lst])
    return (y_p.reshape(bp, sp, d_model), y_s.reshape(bd, 1, d_model),
            stk(p_new, 0), stk(p_new, 1), stk(p_new, 2), stk(p_new, 3), stk(p_new, 4),
            stk(s_new, 0), stk(s_new, 1), stk(s_new, 2), stk(s_new, 3), stk(s_new, 4))
```

```python
import functools
import math

import jax
import jax.numpy as jnp
from jax import lax
from jax.experimental import pallas as pl
from jax.experimental.pallas import tpu as pltpu

F32 = jnp.float32
BF16 = jnp.bfloat16
NEG_INF = float("-inf")

LANES = 128
SUBLANES = 8
DA_HEADS = 4
DA_HD = 64
DA_WIDTH = DA_HEADS * 2 * DA_HD
SA_HEADS = 8
SA_HD = 64
SA_WIDTH = SA_HEADS * SA_HD
IDX_HEADS = 8
IDX_HD = 64
TOPK_MAX = 256
PEER_HEADS = 8
PEER_NKEYS = 128
PEER_QD = 256
PEER_TOPK = 16
LN_EPS = 1e-5
RMS_EPS = 1e-5
VMEM_LIMIT = 56 * 1024 * 1024


def _alibi_slopes(n):
    return [2.0 ** (-8.0 * (i + 1) / n) for i in range(n)]


def _dot(a, b):
    return jnp.dot(a, b, preferred_element_type=F32)


def _dot_nt(a, b):
    return lax.dot_general(a, b, (((1,), (1,)), ((), ())), preferred_element_type=F32)


def _fold_lanes(x, op):
    parts = [x[:, t * LANES:(t + 1) * LANES] for t in range(x.shape[1] // LANES)]
    return functools.reduce(op, parts)


def _params(sem):
    return pltpu.CompilerParams(dimension_semantics=sem, vmem_limit_bytes=VMEM_LIMIT)


def _layer_norm(x, g, b):
    mu = jnp.mean(x, axis=-1, keepdims=True)
    xc = x - mu
    var = jnp.mean(xc * xc, axis=-1, keepdims=True)
    return xc * lax.rsqrt(var + LN_EPS) * g + b


_SEG = dict(qa=0, ka=512, va=1024, qs=1536, ks=2048, vs=2560, qi=3072, ki=3584, wi=3712, ga=3840)


def _arrange_w_in(w_in, d_model):
    offs = [0]
    for n in (DA_WIDTH, DA_WIDTH, DA_WIDTH, SA_WIDTH, SA_WIDTH, SA_WIDTH,
              IDX_HEADS * IDX_HD, IDX_HD, IDX_HEADS, d_model, d_model):
        offs.append(offs[-1] + n)
    ki = w_in[:, offs[7]:offs[8]]
    wi = w_in[:, offs[8]:offs[9]]
    pad = jnp.zeros((w_in.shape[0], LANES - IDX_HEADS), w_in.dtype)
    return jnp.concatenate([w_in[:, :offs[7]], ki, ki, wi, pad, w_in[:, offs[9]:]], axis=1).astype(BF16)


def _in_proj_kernel(x_ref, w_ref, qa_o, kaf_o, kab_o, vaf_o, vab_o, qs_o, ksf_o, ksb_o, vsf_o, vsb_o,
                    qi_o, kif_o, ki2_o, wi_o, ga_o, gb_o, *, d_model, transposed):
    xb = x_ref[...].astype(BF16)
    cache = (lambda z: z.T) if transposed else (lambda z: z)

    def seg(name, n):
        a = _SEG[name]
        return _dot(xb, w_ref[:, a:a + n])

    qa_o[...] = (seg("qa", DA_WIDTH) * (DA_HD ** -0.5)).astype(BF16)
    z = seg("ka", DA_WIDTH)
    kaf_o[...] = cache(z)
    kab_o[...] = z.astype(BF16)
    z = seg("va", DA_WIDTH)
    vaf_o[...] = z
    vab_o[...] = z.astype(BF16)
    qs_o[...] = (seg("qs", SA_WIDTH) * (SA_HD ** -0.5)).astype(BF16)
    z = seg("ks", SA_WIDTH)
    ksf_o[...] = cache(z)
    ksb_o[...] = z.astype(BF16)
    z = seg("vs", SA_WIDTH)
    vsf_o[...] = cache(z)
    vsb_o[...] = z.astype(BF16)
    qi_o[...] = (seg("qi", IDX_HEADS * IDX_HD) * (IDX_HD ** -0.5)).astype(BF16)
    z = seg("ki", 2 * IDX_HD)
    kif_o[...] = z.T[:IDX_HD, :] if transposed else z[:, :IDX_HD]
    ki2_o[...] = z.astype(BF16)
    wi_o[...] = seg("wi", LANES) * (IDX_HEADS ** -0.5)
    ga_o[...] = seg("ga", d_model)
    gb_o[...] = _dot(xb, w_ref[:, _SEG["ga"] + d_model:_SEG["ga"] + 2 * d_model])


def _in_proj(x, w_arr, seq=None):
    t, d_model = x.shape
    tm = min(t, 256)
    assert t % tm == 0
    row = lambda n: pl.BlockSpec((tm, n), lambda i: (i, 0))
    widths = [(512, BF16), (512, F32), (512, BF16), (512, F32), (512, BF16), (512, BF16), (512, F32),
              (512, BF16), (512, F32), (512, BF16), (512, BF16), (IDX_HD, F32), (LANES, BF16), (LANES, F32),
              (d_model, F32), (d_model, F32)]
    shapes = [jax.ShapeDtypeStruct((t, n), dt) for n, dt in widths]
    specs = [row(n) for n, _ in widths]
    if seq is not None:
        assert seq % tm == 0
        per = seq // tm
        for j in (1, 6, 8, 11):
            n = widths[j][0]
            shapes[j] = jax.ShapeDtypeStruct((t // seq, n, seq), F32)
            specs[j] = pl.BlockSpec((None, n, tm), lambda i: (i // per, 0, i % per))
    return pl.pallas_call(
        functools.partial(_in_proj_kernel, d_model=d_model, transposed=seq is not None),
        out_shape=shapes,
        grid=(t // tm,),
        in_specs=[row(d_model), pl.BlockSpec(w_arr.shape, lambda i: (0, 0))],
        out_specs=specs,
        compiler_params=_params(("parallel",)),
        name="in_proj",
    )(x, w_arr)


_ORDERED_NEG_INF = -(2 ** 31) + 0x7FFFFF
_ORDERED_POS_INF = 0x7F800000


def _float_of_ordered(k):
    return lax.bitcast_convert_type(k ^ ((k >> 31) & jnp.int32(0x7FFFFFFF)), F32)


def _topk_mask(key_scr, msk_scr, nk, k_sel, idx_bits):
    cw, rows = key_scr.shape[1], key_scr.shape[2]
    pos = lax.broadcasted_iota(jnp.int32, (cw, rows), 0)

    def count(pred_fn):
        def body(c, acc):
            hit = jnp.where(pred_fn(key_scr[c], pos + c * cw), 1.0, 0.0)
            parts = [hit[g * SUBLANES:(g + 1) * SUBLANES, :] for g in range(cw // SUBLANES)]
            while len(parts) > 1:
                parts = [parts[g] + parts[g + 1] for g in range(0, len(parts) - 1, 2)] + parts[len(parts) & ~1:]
            return acc + parts[0]
        acc = lax.fori_loop(0, nk, body, jnp.zeros((SUBLANES, rows), F32))
        return jnp.sum(acc, axis=0, keepdims=True)

    sign = jnp.int32(-2 ** 31)
    max_off = jnp.int32((_ORDERED_POS_INF - _ORDERED_NEG_INF - 2 ** 31))

    def thr_bit(it, off):
        cand = off + lax.shift_left(jnp.int32(1), 31 - it)
        ok = (cand ^ sign) <= max_off
        t = _float_of_ordered(jnp.where(ok, jnp.int32(_ORDERED_NEG_INF) + cand, jnp.int32(_ORDERED_POS_INF)))
        cnt = count(lambda key, p: key >= t)
        return jnp.where(ok & (cnt >= k_sel), cand, off)

    off = lax.fori_loop(0, 32, thr_bit, jnp.zeros((1, rows), jnp.int32))
    thr = _float_of_ordered(jnp.int32(_ORDERED_NEG_INF) + off)
    need = k_sel - count(lambda key, p: key > thr)

    def lim_bit(it, lim):
        cand = lim + lax.shift_left(jnp.int32(1), idx_bits - 1 - it)
        cnt = count(lambda key, p: (key == thr) & (p < cand))
        return jnp.where(cnt < need, cand, lim)

    tied = jnp.max(count(lambda key, p: key >= thr)) > k_sel
    lim = lax.cond(tied,
                   lambda: lax.fori_loop(0, idx_bits, lim_bit, jnp.zeros((1, rows), jnp.int32)),
                   lambda: jnp.full((1, rows), (1 << idx_bits) - 1, jnp.int32))

    def write(c, carry):
        key = key_scr[c]
        sel = (key > thr) | ((key == thr) & (pos + c * cw <= lim))
        msk_scr[c] = jnp.where(sel, 0.0, NEG_INF).T
        return carry

    lax.fori_loop(0, nk, write, 0)


def _attend(units, q_of, kcols, vcols, slopes, k_ref, v_ref, msk_scr, s_scr, m_scr, l_scr, acc_scr,
            q0, nk, sq, ck):
    rel = (lax.broadcasted_iota(jnp.int32, (sq, ck), 0)
           - lax.broadcasted_iota(jnp.int32, (sq, ck), 1)).astype(F32)
    m_scr[...] = jnp.full(m_scr.shape, NEG_INF, F32)
    l_scr[...] = jnp.zeros(l_scr.shape, F32)
    acc_scr[...] = jnp.zeros(acc_scr.shape, F32)

    def pass1(kc, carry):
        k0 = pl.multiple_of(kc * ck, ck)
        dist = (q0 - k0).astype(F32) + rel
        ok = dist >= 0.0
        extra = None if msk_scr is None else msk_scr[kc]
        for u in units:
            kb = k_ref[pl.ds(k0, ck), kcols(u)]
            s = _dot_nt(q_of(u), kb) - slopes[u] * dist
            if extra is not None:
                s = s + extra
            s = jnp.where(ok, s, NEG_INF)
            s_scr[u, kc] = s
            m_scr[u] = jnp.maximum(m_scr[u], _fold_lanes(s, jnp.maximum))
        return carry

    lax.fori_loop(0, nk, pass1, 0)
    m_col = [jnp.max(m_scr[u], axis=1, keepdims=True) for u in units]

    def pass2(kc, carry):
        k0 = pl.multiple_of(kc * ck, ck)
        for u in units:
            p = jnp.exp(s_scr[u, kc] - m_col[u])
            l_scr[u] = l_scr[u] + _fold_lanes(p, jnp.add)
            acc_scr[u] = acc_scr[u] + _dot(p.astype(BF16), v_ref[pl.ds(k0, ck), vcols(u)])
        return carry

    lax.fori_loop(0, nk, pass2, 0)


def _half_masks(shape):
    lane = lax.broadcasted_iota(jnp.int32, shape, 1)
    return lane < (LANES // 2), lane >= (LANES // 2)


def _diff_lambda(lv, lam_init):
    d1 = jnp.sum(lv[0:1, :] * lv[1:2, :], axis=1, keepdims=True)
    d2 = jnp.sum(lv[2:3, :] * lv[3:4, :], axis=1, keepdims=True)
    return jnp.exp(d1) - jnp.exp(d2) + lam_init


def _diff_combine(o0, o1, lam, g, lam_init):
    o = o0 - lam * o1
    o = o * lax.rsqrt(jnp.mean(o * o, axis=-1, keepdims=True) + RMS_EPS)
    return o * g * (1.0 - lam_init)


def _diff_attn_kernel(lv_ref, g_ref, q_ref, k_ref, v_ref, o_ref, s_scr, m_scr, l_scr, acc_scr,
                      *, sq, ck, lam_init):
    j = pl.program_id(1)
    q0 = j * sq
    nk = (q0 + sq - 1) // ck + 1
    lo, hi = _half_masks((sq, LANES))
    qm = []
    for h in range(DA_HEADS):
        qh = q_ref[:, h * LANES:(h + 1) * LANES]
        qm.append(jnp.where(lo, qh, jnp.zeros_like(qh)))
        qm.append(jnp.where(hi, qh, jnp.zeros_like(qh)))
    units = list(range(2 * DA_HEADS))
    cols = lambda u: slice((u // 2) * LANES, (u // 2 + 1) * LANES)
    slopes = [s for s in _alibi_slopes(DA_HEADS) for _ in range(2)]
    _attend(units, lambda u: qm[u], cols, cols, slopes, k_ref, v_ref, None, s_scr, m_scr, l_scr, acc_scr,
            q0, nk, sq, ck)
    lam = _diff_lambda(lv_ref[...], lam_init)
    for h in range(DA_HEADS):
        o0 = acc_scr[2 * h] / jnp.sum(l_scr[2 * h], axis=1, keepdims=True)
        o1 = acc_scr[2 * h + 1] / jnp.sum(l_scr[2 * h + 1], axis=1, keepdims=True)
        o = _diff_combine(o0, o1, lam, g_ref[...], lam_init)
        o_ref[:, h * LANES:(h + 1) * LANES] = o.astype(o_ref.dtype)


def _prompt_diff_attn(lv, g, qa, ka, va, lam_init):
    b, s, _ = qa.shape
    sq = min(s, 128)
    ck = min(s, 512)
    nc = s // ck
    u = 2 * DA_HEADS
    blk_q = pl.BlockSpec((None, sq, DA_WIDTH), lambda bi, j: (bi, j, 0))
    blk_kv = pl.BlockSpec((None, s, DA_WIDTH), lambda bi, j: (bi, 0, 0))
    full = lambda a: pl.BlockSpec(a.shape, lambda bi, j: (0,) * a.ndim)
    return pl.pallas_call(
        functools.partial(_diff_attn_kernel, sq=sq, ck=ck, lam_init=lam_init),
        out_shape=jax.ShapeDtypeStruct((b, s, DA_WIDTH), BF16),
        grid=(b, s // sq),
        in_specs=[full(lv), full(g), blk_q, blk_kv, blk_kv],
        out_specs=blk_q,
        scratch_shapes=[pltpu.VMEM((u, nc, sq, ck), F32), pltpu.VMEM((u, sq, LANES), F32),
                        pltpu.VMEM((u, sq, LANES), F32), pltpu.VMEM((u, sq, LANES), F32)],
        compiler_params=_params(("parallel", "arbitrary")),
        name="prompt_diff_attn",
    )(lv, g, qa, ka, va)


def _index_scores_chunk(qim, wi_t, kblk):
    total = None
    for h in range(IDX_HEADS):
        r = jnp.maximum(_dot_nt(kblk, qim[h]), 0.0)
        term = wi_t[h:h + 1, :] * r
        total = term if total is None else total + term
    return total


def _sparse_attn_kernel(qi_ref, wi_ref, ki_ref, qs_ref, ks_ref, vs_ref, o_ref,
                        key_scr, msk_scr, s_scr, m_scr, l_scr, acc_scr, *, sq, ck, k_sel, idx_bits):
    j = pl.program_id(1)
    q0 = j * sq
    nk = (q0 + sq - 1) // ck + 1
    lo, hi = _half_masks((sq, LANES))

    def masked(ref):
        out = []
        for p in range(ref.shape[1] // LANES):
            blk = ref[:, p * LANES:(p + 1) * LANES]
            out.append(jnp.where(lo, blk, jnp.zeros_like(blk)))
            out.append(jnp.where(hi, blk, jnp.zeros_like(blk)))
        return out

    qim = masked(qi_ref)
    wi_t = wi_ref[...]
    rel = (lax.broadcasted_iota(jnp.int32, (ck, sq), 1) - lax.broadcasted_iota(jnp.int32, (ck, sq), 0))

    def scores(kc, carry):
        k0 = pl.multiple_of(kc * ck, ck)
        tot = _index_scores_chunk(qim, wi_t, ki_ref[pl.ds(k0, ck), :])
        valid = (q0 - k0) + rel >= 0
        key_scr[kc] = jnp.where(valid, tot, NEG_INF)
        return carry

    lax.fori_loop(0, nk, scores, 0)
    _topk_mask(key_scr, msk_scr, nk, k_sel, idx_bits)

    qsm = masked(qs_ref)
    units = list(range(SA_HEADS))
    cols = lambda u: slice((u // 2) * LANES, (u // 2 + 1) * LANES)
    _attend(units, lambda u: qsm[u], cols, cols, _alibi_slopes(SA_HEADS), ks_ref, vs_ref, msk_scr,
            s_scr, m_scr, l_scr, acc_scr, q0, nk, sq, ck)
    for p in range(SA_HEADS // 2):
        oa = acc_scr[2 * p] / jnp.sum(l_scr[2 * p], axis=1, keepdims=True)
        ob = acc_scr[2 * p + 1] / jnp.sum(l_scr[2 * p + 1], axis=1, keepdims=True)
        o_ref[:, p * LANES:(p + 1) * LANES] = jnp.where(lo, oa, ob).astype(o_ref.dtype)


def _prompt_sparse_attn(qi, wi_t, ki2, qs, ks, vs):
    b, s, _ = qs.shape
    sq = min(s, 256)
    ck = min(s, 512)
    nc = s // ck
    k_sel = min(TOPK_MAX, s // 4)
    idx_bits = max(1, (s - 1).bit_length())
    blk_q = lambda w: pl.BlockSpec((None, sq, w), lambda bi, j: (bi, j, 0))
    blk_kv = lambda w: pl.BlockSpec((None, s, w), lambda bi, j: (bi, 0, 0))
    u = SA_HEADS
    return pl.pallas_call(
        functools.partial(_sparse_attn_kernel, sq=sq, ck=ck, k_sel=k_sel, idx_bits=idx_bits),
        out_shape=jax.ShapeDtypeStruct((b, s, SA_WIDTH), BF16),
        grid=(b, s // sq),
        in_specs=[blk_q(IDX_HEADS * IDX_HD), pl.BlockSpec((None, IDX_HEADS, sq), lambda bi, j: (bi, 0, j)),
                  blk_kv(LANES), blk_q(SA_WIDTH), blk_kv(SA_WIDTH), blk_kv(SA_WIDTH)],
        out_specs=blk_q(SA_WIDTH),
        scratch_shapes=[pltpu.VMEM((nc, ck, sq), F32), pltpu.VMEM((nc, sq, ck), F32),
                        pltpu.VMEM((u, nc, sq, ck), F32), pltpu.VMEM((u, sq, LANES), F32),
                        pltpu.VMEM((u, sq, LANES), F32), pltpu.VMEM((u, sq, LANES), F32)],
        compiler_params=_params(("parallel", "arbitrary")),
        name="prompt_sparse_attn",
    )(qi, wi_t, ki2, qs, ks, vs)


def _sample_index_kernel(pt_ref, qi_ref, wi_ref, kin_ref, cache_ref, o_ref, buf, sem,
                         *, layer, sb, n_pages, page):
    g = pl.program_id(0)
    ng = pl.num_programs(0)
    past = n_pages * page

    def copies(step, slot):
        out = []
        for s in range(sb):
            for p in range(n_pages):
                src = cache_ref.at[layer, pt_ref[step * sb + s, p]]
                out.append(pltpu.make_async_copy(src, buf.at[slot, s, :, pl.ds(p * page, page)], sem.at[slot]))
        return out

    slot = g % 2

    @pl.when(g == 0)
    def _():
        for c in copies(0, 0):
            c.start()

    @pl.when(g + 1 < ng)
    def _():
        for c in copies(g + 1, 1 - slot):
            c.start()

    for c in copies(g, slot):
        c.wait()

    lane = lax.broadcasted_iota(jnp.int32, (1, page), 1)
    for s in range(sb):
        q = qi_ref[s]
        w = wi_ref[s]
        r = jnp.maximum(_dot(q, buf[slot, s].astype(BF16)), 0.0)
        o_ref[s:s + 1, 0:past] = jnp.sum(w * r, axis=0, keepdims=True)
        kn = kin_ref[s].astype(BF16).astype(F32)
        rn = jnp.maximum(jnp.sum(q.astype(F32) * kn, axis=1, keepdims=True), 0.0)
        tot = jnp.sum(w * rn, axis=0, keepdims=True)
        o_ref[s:s + 1, past:past + page] = jnp.where(lane == 0, tot, NEG_INF)


def _sample_index_scores(page_table, qi8, wi8, ki_new, idx_kt, layer):
    bd, n_pages = page_table.shape
    page = idx_kt.shape[3]
    sb = min(bd, SUBLANES)
    assert bd % sb == 0
    lpad = (n_pages + 1) * page
    grid_spec = pltpu.PrefetchScalarGridSpec(
        num_scalar_prefetch=1,
        grid=(bd // sb,),
        in_specs=[
            pl.BlockSpec((sb, IDX_HEADS, IDX_HD), lambda g, pt: (g, 0, 0)),
            pl.BlockSpec((sb, IDX_HEADS, 1), lambda g, pt: (g, 0, 0)),
            pl.BlockSpec((sb, 1, IDX_HD), lambda g, pt: (g, 0, 0)),
            pl.BlockSpec(memory_space=pl.ANY),
        ],
        out_specs=pl.BlockSpec((sb, lpad), lambda g, pt: (g, 0)),
        scratch_shapes=[pltpu.VMEM((2, sb, IDX_HD, n_pages * page), F32), pltpu.SemaphoreType.DMA((2,))],
    )
    return pl.pallas_call(
        functools.partial(_sample_index_kernel, layer=layer, sb=sb, n_pages=n_pages, page=page),
        out_shape=jax.ShapeDtypeStruct((bd, lpad), F32),
        grid_spec=grid_spec,
        compiler_params=_params(("arbitrary",)),
        name="sample_index_scores",
    )(page_table, qi8, wi8, ki_new, idx_kt)


def _sample_select_kernel(sc_ref, msk_ref, key_scr, msk_scr, *, k_sel, cw, idx_bits):
    nc = key_scr.shape[0]
    for c in range(nc):
        key_scr[c] = sc_ref[:, c * cw:(c + 1) * cw].T
    _topk_mask(key_scr, msk_scr, nc, k_sel, idx_bits)
    for c in range(nc):
        msk_ref[:, c * cw:(c + 1) * cw] = msk_scr[c]


def _sample_select(scores, k_sel, cw):
    bd, lpad = scores.shape
    nc = lpad // cw
    idx_bits = max(1, (lpad - 1).bit_length())
    return pl.pallas_call(
        functools.partial(_sample_select_kernel, k_sel=k_sel, cw=cw, idx_bits=idx_bits),
        out_shape=jax.ShapeDtypeStruct((bd, lpad), F32),
        grid=(1,),
        in_specs=[pl.BlockSpec((bd, lpad), lambda i: (0, 0))],
        out_specs=pl.BlockSpec((bd, lpad), lambda i: (0, 0)),
        scratch_shapes=[pltpu.VMEM((nc, cw, bd), F32), pltpu.VMEM((nc, bd, cw), F32)],
        compiler_params=_params(("arbitrary",)),
        name="sample_select",
    )(scores)


def _online_update(s, pv_fn, m_scr, l_scr, acc_scr):
    m_old = m_scr[...]
    m_new = jnp.maximum(m_old, jnp.max(s, axis=1, keepdims=True))
    m_safe = jnp.where(m_new == NEG_INF, 0.0, m_new)
    alpha = jnp.exp(m_old - m_safe)
    p = jnp.exp(s - m_safe)
    l_scr[...] = alpha * l_scr[...] + jnp.sum(p, axis=1, keepdims=True)
    acc_scr[...] = alpha * acc_scr[...] + pv_fn(p.astype(BF16))
    m_scr[...] = m_new


def _sample_attn_kernel(pt_ref, lv_ref, g_ref, qa_ref, qs_ref, kan_ref, van_ref, ksn_ref, vsn_ref,
                        msk_ref, mskn_ref, dak_hbm, dav_hbm, sak_hbm, sav_hbm, a_ref, o_ref,
                        dak_buf, dav_buf, sak_buf, sav_buf, sem, qda_scr, qsa_scr, m_scr, l_scr, acc_scr,
                        *, layer, n_pages, pg, page, lam_init):
    b = pl.program_id(0)
    hf = pl.program_id(1)
    nh = pl.num_programs(1)
    step = b * nh + hf
    n_steps = pl.num_programs(0) * nh
    slot = step % 2
    past = n_pages * page
    span = pg * page

    def copies(bb, hh, sl):
        out = []
        for j in range(pg):
            pid = pt_ref[bb, hh * pg + j]
            cols = pl.ds(j * page, page)
            out.append(pltpu.make_async_copy(dak_hbm.at[layer, pid], dak_buf.at[sl, :, cols], sem.at[sl, 0]))
            out.append(pltpu.make_async_copy(dav_hbm.at[layer, pid],
                                             dav_buf.at[sl, pl.ds(j * page * DA_HEADS, page * DA_HEADS), :],
                                             sem.at[sl, 1]))
            out.append(pltpu.make_async_copy(sak_hbm.at[layer, pid], sak_buf.at[sl, :, cols], sem.at[sl, 2]))
            out.append(pltpu.make_async_copy(sav_hbm.at[layer, pid], sav_buf.at[sl, :, cols], sem.at[sl, 3]))
        return out

    @pl.when(step == 0)
    def _():
        for c in copies(0, 0, 0):
            c.start()

    @pl.when(step + 1 < n_steps)
    def _():
        wrap = hf + 1 == nh
        for c in copies(jnp.where(wrap, b + 1, b), jnp.where(wrap, 0, hf + 1), 1 - slot):
            c.start()

    row = lax.broadcasted_iota(jnp.int32, (SUBLANES, DA_WIDTH), 0)
    col = lax.broadcasted_iota(jnp.int32, (SUBLANES, DA_WIDTH), 1)
    row1 = lax.broadcasted_iota(jnp.int32, (SUBLANES, 1), 0)
    slope_da = jnp.zeros((SUBLANES, 1), F32)
    for h, sl in enumerate(_alibi_slopes(DA_HEADS)):
        slope_da = jnp.where(row1 // 2 == h, sl, slope_da)
    slope_sa = jnp.zeros((SUBLANES, 1), F32)
    for h, sl in enumerate(_alibi_slopes(SA_HEADS)):
        slope_sa = jnp.where(row1 == h, sl, slope_sa)

    @pl.when(hf == 0)
    def _():
        qa = jnp.broadcast_to(qa_ref[...].astype(F32), (SUBLANES, DA_WIDTH))
        qda_scr[...] = jnp.where(col // DA_HD == row, qa, 0.0)
        qs = jnp.broadcast_to(qs_ref[...].astype(F32), (SUBLANES, SA_WIDTH))
        qsa_scr[...] = jnp.where(col // SA_HD == row, qs, 0.0)
        m_scr[...] = jnp.full(m_scr.shape, NEG_INF, F32)
        l_scr[...] = jnp.zeros(l_scr.shape, F32)
        acc_scr[...] = jnp.zeros(acc_scr.shape, F32)

    for c in copies(b, hf, slot):
        c.wait()

    kpos = hf * span + lax.broadcasted_iota(jnp.int32, (1, span), 1)
    dist = (past - kpos).astype(F32)

    def pv_da(p):
        return jnp.concatenate(
            [_dot(p, dav_buf[slot, pl.ds(h, span, stride=DA_HEADS), :].astype(BF16)) for h in range(DA_HEADS)],
            axis=1)

    s = _dot(qda_scr[...].astype(BF16), dak_buf[slot].astype(BF16)) - slope_da * dist
    _online_update(s, pv_da, m_scr.at[0], l_scr.at[0], acc_scr.at[0])
    s = _dot(qsa_scr[...].astype(BF16), sak_buf[slot].astype(BF16)) - slope_sa * dist + msk_ref[...]
    _online_update(s, lambda p: _dot_nt(p, sav_buf[slot].astype(BF16)), m_scr.at[1], l_scr.at[1], acc_scr.at[1])

    @pl.when(hf == nh - 1)
    def _():
        rnd = lambda ref: ref[...].astype(BF16).astype(F32)
        s = jnp.sum(qda_scr[...] * rnd(kan_ref), axis=1, keepdims=True)
        _online_update(s, lambda p: p.astype(F32) * rnd(van_ref), m_scr.at[0], l_scr.at[0], acc_scr.at[0])
        s = jnp.sum(qsa_scr[...] * rnd(ksn_ref), axis=1, keepdims=True) + mskn_ref[:, 0:1]
        _online_update(s, lambda p: p.astype(F32) * rnd(vsn_ref), m_scr.at[1], l_scr.at[1], acc_scr.at[1])

        lam = _diff_lambda(lv_ref[...], lam_init)
        oda = acc_scr[0] / l_scr[0]
        for h in range(DA_HEADS):
            cs = slice(h * LANES, (h + 1) * LANES)
            o = _diff_combine(oda[2 * h:2 * h + 1, cs], oda[2 * h + 1:2 * h + 2, cs], lam, g_ref[...], lam_init)
            a_ref[:, cs] = o.astype(a_ref.dtype)
        osa = acc_scr[1] / l_scr[1]
        lo, _ = _half_masks((1, LANES))
        for pr in range(SA_HEADS // 2):
            cs = slice(pr * LANES, (pr + 1) * LANES)
            o_ref[:, cs] = jnp.where(lo, osa[2 * pr:2 * pr + 1, cs], osa[2 * pr + 1:2 * pr + 2, cs]).astype(o_ref.dtype)


def _sample_attn(page_table, lv, g, qa, qs, ka_new, va_new, ks_new, vs_new, msk, da_kt, da_v, sa_kt, sa_vt,
                 layer, lam_init):
    bd, n_pages = page_table.shape
    page = da_kt.shape[3]
    pg = math.gcd(n_pages, 8)
    nh = n_pages // pg
    span = pg * page
    tok = lambda w: pl.BlockSpec((None, 1, w), lambda b, hf, pt: (b, 0, 0))
    full = lambda a: pl.BlockSpec(a.shape, lambda b, hf, pt: (0,) * a.ndim)
    hbm = pl.BlockSpec(memory_space=pl.ANY)
    grid_spec = pltpu.PrefetchScalarGridSpec(
        num_scalar_prefetch=1,
        grid=(bd, nh),
        in_specs=[full(lv), full(g), tok(DA_WIDTH), tok(SA_WIDTH), tok(DA_WIDTH), tok(DA_WIDTH),
                  tok(SA_WIDTH), tok(SA_WIDTH),
                  pl.BlockSpec((None, 1, span), lambda b, hf, pt: (b, 0, hf)),
                  pl.BlockSpec((None, 1, page), lambda b, hf, pt: (b, 0, n_pages)),
                  hbm, hbm, hbm, hbm],
        out_specs=[tok(DA_WIDTH), tok(SA_WIDTH)],
        scratch_shapes=[pltpu.VMEM((2, DA_WIDTH, span), F32), pltpu.VMEM((2, span * DA_HEADS, LANES), F32),
                        pltpu.VMEM((2, SA_WIDTH, span), F32), pltpu.VMEM((2, SA_WIDTH, span), F32),
                        pltpu.SemaphoreType.DMA((2, 4)),
                        pltpu.VMEM((SUBLANES, DA_WIDTH), F32), pltpu.VMEM((SUBLANES, SA_WIDTH), F32),
                        pltpu.VMEM((2, SUBLANES, 1), F32), pltpu.VMEM((2, SUBLANES, 1), F32),
                        pltpu.VMEM((2, SUBLANES, DA_WIDTH), F32)],
    )
    return pl.pallas_call(
        functools.partial(_sample_attn_kernel, layer=layer, n_pages=n_pages, pg=pg, page=page, lam_init=lam_init),
        out_shape=[jax.ShapeDtypeStruct((bd, 1, DA_WIDTH), BF16), jax.ShapeDtypeStruct((bd, 1, SA_WIDTH), BF16)],
        grid_spec=grid_spec,
        compiler_params=_params(("arbitrary", "arbitrary")),
        name="sample_attn",
    )(page_table, lv, g, qa, qs, ka_new, va_new, ks_new, vs_new, msk, msk, da_kt, da_v, sa_kt, sa_vt)


def _merge_kernel(a_ref, o_ref, ga_ref, gb_ref, x_ref, wa_ref, wb_ref, wo_ref, g_ref, b_ref, wq_ref,
                  h_ref, q_ref, *, alpha):
    merged = (jax.nn.sigmoid(ga_ref[...]) * _dot(a_ref[...], wa_ref[...])
              + jax.nn.sigmoid(gb_ref[...]) * _dot(o_ref[...], wb_ref[...]))
    h = _layer_norm(alpha * x_ref[...] + _dot(merged.astype(BF16), wo_ref[...]), g_ref[...], b_ref[...])
    h_ref[...] = h
    q = _dot(h.astype(BF16), wq_ref[...]).astype(q_ref.dtype)
    for hd in range(PEER_HEADS):
        q_ref[hd] = q[:, hd * PEER_QD:(hd + 1) * PEER_QD]


def _merge(a, o, ga, gb, x, wa, wb, wo, g, b, wq, alpha):
    t, d_model = x.shape
    tm = min(t, 256)
    row = lambda n: pl.BlockSpec((tm, n), lambda i: (i, 0))
    full = lambda arr: pl.BlockSpec(arr.shape, lambda i: (0,) * arr.ndim)
    return pl.pallas_call(
        functools.partial(_merge_kernel, alpha=alpha),
        out_shape=[jax.ShapeDtypeStruct((t, d_model), F32), jax.ShapeDtypeStruct((PEER_HEADS, t, PEER_QD), BF16)],
        grid=(t // tm,),
        in_specs=[row(DA_WIDTH), row(SA_WIDTH), row(d_model), row(d_model), row(d_model),
                  full(wa), full(wb), full(wo), full(g), full(b), full(wq)],
        out_specs=[row(d_model), pl.BlockSpec((PEER_HEADS, tm, PEER_QD), lambda i: (0, i, 0))],
        compiler_params=_params(("parallel",)),
        name="merge_ln_peerq",
    )(a, o, ga, gb, x, wa, wb, wo, g, b, wq)


def _take_max(s, ridx, exact):
    m = jnp.max(s, axis=0, keepdims=True)
    if not exact:
        return m, s == m
    first = jnp.min(jnp.where(s == m, ridx, float(s.shape[0])), axis=0, keepdims=True)
    return m, ridx == first


def _count_mismatch(flag_arrays, want):
    off = [jnp.abs(jnp.sum(f, axis=0, keepdims=True) - float(want)) for f in flag_arrays]
    return jnp.max(functools.reduce(jnp.add, off)) > 0.0


_CAND_LIM = [PEER_TOPK // (a + 1) for a in range(SUBLANES)]


def _top_subkeys(s0, s1, r0_scr, r1_scr, v_scr, exact):
    topk = PEER_TOPK
    rows, t = s0.shape
    ridx = lax.broadcasted_iota(jnp.int32, (rows, t), 0).astype(F32)
    kidx = lax.broadcasted_iota(jnp.int32, (topk, t), 0)
    r0_scr[...] = jnp.full((rows, t), float(topk), F32)
    r1_scr[...] = jnp.full((rows, t), float(topk), F32)

    def body(k, carry):
        c0, c1, v0, v1 = carry
        kf = jnp.asarray(k).astype(F32)
        m0, hit0 = _take_max(c0, ridx, exact)
        m1, hit1 = _take_max(c1, ridx, exact)
        r0_scr[...] = jnp.where(hit0, kf, r0_scr[...])
        r1_scr[...] = jnp.where(hit1, kf, r1_scr[...])
        return (jnp.where(hit0, NEG_INF, c0), jnp.where(hit1, NEG_INF, c1),
                jnp.where(kidx == k, m0, v0), jnp.where(kidx == k, m1, v1))

    zero = jnp.zeros((topk, t), F32)
    _, _, v0, v1 = lax.fori_loop(0, topk, body, (s0, s1, zero, zero))
    v_scr[0] = v0
    v_scr[1] = v1


def _top_candidates(cand, taken_scr, tv_scr, exact):
    topk = PEER_TOPK
    crows, t = cand.shape
    cidx = lax.broadcasted_iota(jnp.int32, (crows, t), 0).astype(F32)
    kidx = lax.broadcasted_iota(jnp.int32, (topk, t), 0)

    def body(k, carry):
        c, taken, tv = carry
        m, hit = _take_max(c, cidx, exact)
        return jnp.where(hit, NEG_INF, c), jnp.where(hit, 1.0, taken), jnp.where(kidx == k, m, tv)

    _, taken, tv = lax.fori_loop(0, topk, body, (cand, jnp.zeros((crows, t), F32), jnp.zeros((topk, t), F32)))
    taken_scr[...] = taken
    tv_scr[...] = tv


def _peer_route_head(q0, q1, sk0, sk1, r0_scr, r1_scr, v_scr, taken_scr, tv_scr):
    topk = PEER_TOPK
    s0 = _dot_nt(sk0, q0)
    s1 = _dot_nt(sk1, q1)
    rows, t = s0.shape

    sub = lax.broadcasted_iota(jnp.int32, (SUBLANES, t), 0)

    def select(exact):
        _top_subkeys(s0, s1, r0_scr, r1_scr, v_scr, exact)
        v0 = v_scr[0]
        v1 = v_scr[1]
        groups = [v0[0:1, :] + v1]
        for a in range(1, SUBLANES):
            groups.append(jnp.where(sub < _CAND_LIM[a], v0[a:a + 1, :] + v1[0:SUBLANES, :], NEG_INF))
        groups.append(v0[SUBLANES:topk, :] + v1[0:1, :])
        cand = jnp.concatenate(groups, axis=0)
        _top_candidates(cand, taken_scr, tv_scr, exact)

    select(exact=False)
    in_top = lambda r_scr: jnp.where(r_scr[...] < float(topk), 1.0, 0.0)

    @pl.when(_count_mismatch([in_top(r0_scr), in_top(r1_scr), taken_scr[...]], topk))
    def _():
        select(exact=True)

    v0 = v_scr[0]
    v1 = v_scr[1]
    taken = taken_scr[...]
    tv = tv_scr[...]
    z = jnp.sum(jnp.exp(tv - tv[0:1, :]), axis=0, keepdims=True)

    r0 = r0_scr[...]
    r1 = r1_scr[...]
    n_sel = jnp.zeros((rows, t), F32)
    for a in range(topk):
        if a == 0:
            n_a = jnp.sum(taken[0:topk, :], axis=0, keepdims=True)
        elif a < SUBLANES:
            base = topk + SUBLANES * (a - 1)
            n_a = jnp.sum(taken[base:base + SUBLANES, :], axis=0, keepdims=True)
        else:
            base = topk + SUBLANES * (SUBLANES - 1) + (a - SUBLANES)
            n_a = taken[base:base + 1, :]
        n_sel = jnp.where(r0 == float(a), n_a, n_sel)
    a_out = jnp.where(r0 < float(topk), jnp.exp(s0 - v0[0:1, :]) / z, 0.0)
    b_out = jnp.where(r1 < float(topk), jnp.exp(s1 - v1[0:1, :]), 0.0)
    return a_out, n_sel, b_out, r1


def _peer_route_kernel(q_ref, sk_ref, a_ref, n_ref, b_ref, r1_ref, r0_scr, r1_scr, v_scr, taken_scr, tv_scr):
    half = PEER_QD // 2

    def head(h, carry):
        a, n, b, r1 = _peer_route_head(q_ref[h, :, :half], q_ref[h, :, half:], sk_ref[h, 0], sk_ref[h, 1],
                                       r0_scr, r1_scr, v_scr, taken_scr, tv_scr)
        a_ref[h] = a
        n_ref[h] = n
        b_ref[h] = b.astype(b_ref.dtype)
        r1_ref[h] = r1.astype(r1_ref.dtype)
        return carry

    lax.fori_loop(0, PEER_HEADS, head, 0)


def _peer_route(q, sub_keys):
    t = q.shape[1]
    tt = min(t, LANES)
    blk = pl.BlockSpec((PEER_HEADS, PEER_NKEYS, tt), lambda i: (0, 0, i))
    shp = lambda dt: jax.ShapeDtypeStruct((PEER_HEADS, PEER_NKEYS, t), dt)
    return pl.pallas_call(
        _peer_route_kernel,
        out_shape=[shp(F32), shp(F32), shp(BF16), shp(BF16)],
        grid=(t // tt,),
        in_specs=[pl.BlockSpec((PEER_HEADS, tt, PEER_QD), lambda i: (0, i, 0)),
                  pl.BlockSpec(sub_keys.shape, lambda i: (0, 0, 0, 0))],
        out_specs=[blk] * 4,
        scratch_shapes=[pltpu.VMEM((PEER_NKEYS, tt), F32), pltpu.VMEM((PEER_NKEYS, tt), F32),
                        pltpu.VMEM((2, PEER_TOPK, tt), F32),
                        pltpu.VMEM((PEER_TOPK + SUBLANES * SUBLANES, tt), F32), pltpu.VMEM((PEER_TOPK, tt), F32)],
        compiler_params=_params(("parallel",)),
        name="peer_route",
    )(q, sub_keys)


def _gelu(x):
    return 0.5 * x * (1.0 + lax.erf(x * (2.0 ** -0.5)))


def _peer_expert_kernel(h_ref, u_ref, vt_ref, a_ref, n_ref, b_ref, r1_ref, g_ref, beta_ref, y_ref,
                        hb_scr, pre_scr, act_scr, acc_scr, *, ci, alpha):
    c = pl.program_id(1)

    @pl.when(c == 0)
    def _():
        hb_scr[...] = h_ref[...].astype(BF16)
        acc_scr[...] = jnp.zeros(acc_scr.shape, F32)

    pre_scr[...] = _dot_nt(u_ref[...], hb_scr[...])
    for i in range(ci):
        rows = slice(i * PEER_NKEYS, (i + 1) * PEER_NKEYS)
        w = None
        for hd in range(PEER_HEADS):
            a_row = a_ref[hd, i:i + 1, :].astype(BF16)
            n_row = n_ref[hd, i:i + 1, :].astype(BF16)
            b = b_ref[hd]
            gate = a_row * jnp.where(r1_ref[hd] < n_row, b, jnp.zeros_like(b))
            w = gate if w is None else w + gate
        act_scr[rows, :] = _gelu(pre_scr[rows, :]).astype(BF16) * w
    acc_scr[...] = acc_scr[...] + _dot(vt_ref[...], act_scr[...])

    @pl.when(c == pl.num_programs(1) - 1)
    def _():
        hv = h_ref[...]
        y_ref[...] = _layer_norm(alpha * hv + acc_scr[...].T, g_ref[...], beta_ref[...])


def _peer_experts(h, u_tab, vt_tab, ra, rn, rb, rr1, g, beta, alpha):
    t, d_model = h.shape
    tt = min(t, 512)
    ci = SUBLANES
    ne = ci * PEER_NKEYS
    nch = u_tab.shape[0] // ne
    sml = pl.BlockSpec((PEER_HEADS, ci, tt), lambda i, c: (0, c, i))
    big = pl.BlockSpec((PEER_HEADS, PEER_NKEYS, tt), lambda i, c: (0, 0, i))
    full = lambda arr: pl.BlockSpec(arr.shape, lambda i, c: (0,) * arr.ndim)
    return pl.pallas_call(
        functools.partial(_peer_expert_kernel, ci=ci, alpha=alpha),
        out_shape=jax.ShapeDtypeStruct((t, d_model), F32),
        grid=(t // tt, nch),
        in_specs=[pl.BlockSpec((tt, d_model), lambda i, c: (i, 0)),
                  pl.BlockSpec((ne, d_model), lambda i, c: (c, 0)),
                  pl.BlockSpec((d_model, ne), lambda i, c: (0, c)),
                  sml, sml, big, big, full(g), full(beta)],
        out_specs=pl.BlockSpec((tt, d_model), lambda i, c: (i, 0)),
        scratch_shapes=[pltpu.VMEM((tt, d_model), BF16), pltpu.VMEM((ne, tt), F32),
                        pltpu.VMEM((ne, tt), BF16), pltpu.VMEM((d_model, tt), F32)],
        compiler_params=_params(("parallel", "arbitrary")),
        name="peer_experts",
    )(h, u_tab, vt_tab, ra, rn, rb, rr1, g, beta)


def _layer_tail(x, a, o, ga, gb, tail, alpha):
    wa, wb, wo, ln1_g, ln1_b, wq, sub_keys, u_tab, vt_tab, ln2_g, ln2_b = tail
    h, q = _merge(a, o, ga, gb, x, wa, wb, wo, ln1_g, ln1_b, wq, alpha)
    ra, rn, rb, rr1 = _peer_route(q, sub_keys)
    return _peer_experts(h, u_tab, vt_tab, ra, rn, rb, rr1, ln2_g, ln2_b, alpha)


def kernel(x_prompt, x_sample, cache_da_k, cache_da_v, cache_sa_k, cache_sa_v, cache_idx_k, page_table, w_in, lambda_q1, lambda_k1, lambda_q2, lambda_k2, da_subln_g, w_a_up, w_b_up, w_out, ln1_g, ln1_b, peer_wq, peer_sub_keys, peer_u, peer_v, ln2_g, ln2_b):
    depth = w_in.shape[0]
    bp, sp, d_model = x_prompt.shape
    bd, sd, _ = x_sample.shape
    assert sd == 1, "the sample pass handles one new token per sequence"
    n_pool, page = cache_da_k.shape[1], cache_da_k.shape[2]
    n_pages = page_table.shape[1]
    alpha = (2.0 * depth) ** 0.25
    row2 = lambda v: v.reshape(1, -1)

    da_kt = cache_da_k.transpose(0, 1, 3, 4, 5, 2).reshape(depth, n_pool, DA_WIDTH, page)
    sa_kt = cache_sa_k.transpose(0, 1, 3, 4, 2).reshape(depth, n_pool, SA_WIDTH, page)
    sa_vt = cache_sa_v.transpose(0, 1, 3, 4, 2).reshape(depth, n_pool, SA_WIDTH, page)
    idx_kt = cache_idx_k.transpose(0, 1, 3, 2)
    da_v = cache_da_v.reshape(depth, n_pool, page * DA_HEADS, 2 * DA_HD)

    y_p = x_prompt.reshape(bp * sp, d_model)
    y_s = x_sample.reshape(bd, d_model)
    p_new, s_new = [], []
    for l in range(depth):
        lam_init = 0.8 - 0.6 * math.exp(-0.3 * l)
        lv = jnp.stack([lambda_q1[l], lambda_k1[l], lambda_q2[l], lambda_k2[l]])
        g_sub = row2(da_subln_g[l])
        w_arr = _arrange_w_in(w_in[l], d_model)
        tail = (w_a_up[l].astype(BF16), w_b_up[l].astype(BF16), w_out[l].astype(BF16), row2(ln1_g[l]),
                row2(ln1_b[l]), peer_wq[l].astype(BF16), peer_sub_keys[l].astype(BF16),
                peer_u[l].astype(BF16), peer_v[l].T.astype(BF16), row2(ln2_g[l]), row2(ln2_b[l]))

        (qa, kat, kab, vaf, vab, qs, kst, ksb, vst, vsb, qi, kit, ki2, wi, ga, gb) = _in_proj(y_p, w_arr, seq=sp)
        r3 = lambda v: v.reshape(bp, sp, v.shape[-1])
        a = _prompt_diff_attn(lv, g_sub, r3(qa), r3(kab), r3(vab), lam_init)
        wi_t = wi[:, :IDX_HEADS].reshape(bp, sp, IDX_HEADS).transpose(0, 2, 1)
        o = _prompt_sparse_attn(r3(qi), wi_t, r3(ki2), r3(qs), r3(ksb), r3(vsb))
        p_new.append((kat.reshape(bp, DA_HEADS, 2, DA_HD, sp).transpose(0, 4, 1, 2, 3),
                      vaf.reshape(bp, sp, DA_HEADS, 2 * DA_HD),
                      kst.reshape(bp, SA_HEADS, SA_HD, sp).transpose(0, 3, 1, 2),
                      vst.reshape(bp, SA_HEADS, SA_HD, sp).transpose(0, 3, 1, 2),
                      kit.transpose(0, 2, 1)))
        y_p = _layer_tail(y_p, a.reshape(bp * sp, DA_WIDTH), o.reshape(bp * sp, SA_WIDTH), ga, gb, tail, alpha)

        (qa, kaf, kab, vaf, vab, qs, ksf, ksb, vsf, vsb, qi, kif, ki2, wi, ga, gb) = _in_proj(y_s, w_arr)
        t3 = lambda v: v.reshape(bd, 1, v.shape[-1])
        scores = _sample_index_scores(page_table, qi.reshape(bd, IDX_HEADS, IDX_HD),
                                      wi[:, :IDX_HEADS].reshape(bd, IDX_HEADS, 1), t3(kif), idx_kt, l)
        k_sel = min(TOPK_MAX, (n_pages * page + 1) // 4)
        msk = _sample_select(scores, k_sel, page)
        a, o = _sample_attn(page_table, lv, g_sub, t3(qa), t3(qs), t3(kaf), t3(vaf), t3(ksf), t3(vsf),
                            msk.reshape(bd, 1, msk.shape[1]), da_kt, da_v, sa_kt, sa_vt, l, lam_init)
        s_new.append((kaf.reshape(bd, 1, DA_HEADS, 2, DA_HD), vaf.reshape(bd, 1, DA_HEADS, 2 * DA_HD),
                      ksf.reshape(bd, 1, SA_HEADS, SA_HD), vsf.reshape(bd, 1, SA_HEADS, SA_HD),
                      kif.reshape(bd, 1, IDX_HD)))
        y_s = _layer_tail(y_s, a.reshape(bd, DA_WIDTH), o.reshape(bd, SA_WIDTH), ga, gb, tail, alpha)

    stk = lambda lst, j: jnp.stack([e[j] for e in lst])
    return (y_p.reshape(bp, sp, d_model), y_s.reshape(bd, 1, d_model),
            stk(p_new, 0), stk(p_new, 1), stk(p_new, 2), stk(p_new, 3), stk(p_new, 4),
            stk(s_new, 0), stk(s_new, 1), stk(s_new, 2), stk(s_new, 3), stk(s_new, 4))
```

```python
import functools
import math

import jax
import jax.numpy as jnp
from jax import lax
from jax.experimental import pallas as pl
from jax.experimental.pallas import tpu as pltpu

F32 = jnp.float32
BF16 = jnp.bfloat16
NEG_INF = float("-inf")

LANES = 128
SUBLANES = 8
DA_HEADS = 4
DA_HD = 64
DA_WIDTH = DA_HEADS * 2 * DA_HD
SA_HEADS = 8
SA_HD = 64
SA_WIDTH = SA_HEADS * SA_HD
IDX_HEADS = 8
IDX_HD = 64
TOPK_MAX = 256
PEER_HEADS = 8
PEER_NKEYS = 128
PEER_QD = 256
PEER_TOPK = 16
LN_EPS = 1e-5
RMS_EPS = 1e-5
VMEM_LIMIT = 56 * 1024 * 1024


def _alibi_slopes(n):
    return [2.0 ** (-8.0 * (i + 1) / n) for i in range(n)]


def _dot(a, b):
    return jnp.dot(a, b, preferred_element_type=F32)


def _dot_nt(a, b):
    return lax.dot_general(a, b, (((1,), (1,)), ((), ())), preferred_element_type=F32)


def _fold_lanes(x, op):
    parts = [x[:, t * LANES:(t + 1) * LANES] for t in range(x.shape[1] // LANES)]
    return functools.reduce(op, parts)


def _params(sem):
    return pltpu.CompilerParams(dimension_semantics=sem, vmem_limit_bytes=VMEM_LIMIT)


def _layer_norm(x, g, b):
    mu = jnp.mean(x, axis=-1, keepdims=True)
    xc = x - mu
    var = jnp.mean(xc * xc, axis=-1, keepdims=True)
    return xc * lax.rsqrt(var + LN_EPS) * g + b


_SEG = dict(qa=0, ka=512, va=1024, qs=1536, ks=2048, vs=2560, qi=3072, ki=3584, wi=3712, ga=3840)


def _arrange_w_in(w_in, d_model):
    offs = [0]
    for n in (DA_WIDTH, DA_WIDTH, DA_WIDTH, SA_WIDTH, SA_WIDTH, SA_WIDTH,
              IDX_HEADS * IDX_HD, IDX_HD, IDX_HEADS, d_model, d_model):
        offs.append(offs[-1] + n)
    ki = w_in[:, offs[7]:offs[8]]
    wi = w_in[:, offs[8]:offs[9]]
    pad = jnp.zeros((w_in.shape[0], LANES - IDX_HEADS), w_in.dtype)
    return jnp.concatenate([w_in[:, :offs[7]], ki, ki, wi, pad, w_in[:, offs[9]:]], axis=1).astype(BF16)


def _in_proj_kernel(x_ref, w_ref, qa_o, kaf_o, kab_o, vaf_o, vab_o, qs_o, ksf_o, ksb_o, vsf_o, vsb_o,
                    qi_o, kif_o, ki2_o, wi_o, ga_o, gb_o, *, d_model, transposed):
    xb = x_ref[...].astype(BF16)
    cache = (lambda z: z.T) if transposed else (lambda z: z)

    def seg(name, n):
        a = _SEG[name]
        return _dot(xb, w_ref[:, a:a + n])

    qa_o[...] = (seg("qa", DA_WIDTH) * (DA_HD ** -0.5)).astype(BF16)
    z = seg("ka", DA_WIDTH)
    kaf_o[...] = cache(z)
    kab_o[...] = z.astype(BF16)
    z = seg("va", DA_WIDTH)
    vaf_o[...] = z
    vab_o[...] = z.astype(BF16)
    qs_o[...] = (seg("qs", SA_WIDTH) * (SA_HD ** -0.5)).astype(BF16)
    z = seg("ks", SA_WIDTH)
    ksf_o[...] = cache(z)
    ksb_o[...] = z.astype(BF16)
    z = seg("vs", SA_WIDTH)
    vsf_o[...] = cache(z)
    vsb_o[...] = z.astype(BF16)
    qi_o[...] = (seg("qi", IDX_HEADS * IDX_HD) * (IDX_HD ** -0.5)).astype(BF16)
    z = seg("ki", 2 * IDX_HD)
    kif_o[...] = z.T[:IDX_HD, :] if transposed else z[:, :IDX_HD]
    ki2_o[...] = z.astype(BF16)
    wi_o[...] = seg("wi", LANES) * (IDX_HEADS ** -0.5)
    ga_o[...] = seg("ga", d_model)
    gb_o[...] = _dot(xb, w_ref[:, _SEG["ga"] + d_model:_SEG["ga"] + 2 * d_model])


def _in_proj(x, w_arr, seq=None):
    t, d_model = x.shape
    tm = min(t, 256)
    assert t % tm == 0
    row = lambda n: pl.BlockSpec((tm, n), lambda i: (i, 0))
    widths = [(512, BF16), (512, F32), (512, BF16), (512, F32), (512, BF16), (512, BF16), (512, F32),
              (512, BF16), (512, F32), (512, BF16), (512, BF16), (IDX_HD, F32), (LANES, BF16), (LANES, F32),
              (d_model, F32), (d_model, F32)]
    shapes = [jax.ShapeDtypeStruct((t, n), dt) for n, dt in widths]
    specs = [row(n) for n, _ in widths]
    if seq is not None:
        assert seq % tm == 0
        per = seq // tm
        for j in (1, 6, 8, 11):
            n = widths[j][0]
            shapes[j] = jax.ShapeDtypeStruct((t // seq, n, seq), F32)
            specs[j] = pl.BlockSpec((None, n, tm), lambda i: (i // per, 0, i % per))
    return pl.pallas_call(
        functools.partial(_in_proj_kernel, d_model=d_model, transposed=seq is not None),
        out_shape=shapes,
        grid=(t // tm,),
        in_specs=[row(d_model), pl.BlockSpec(w_arr.shape, lambda i: (0, 0))],
        out_specs=specs,
        compiler_params=_params(("parallel",)),
        name="in_proj",
    )(x, w_arr)


_ORDERED_NEG_INF = -(2 ** 31) + 0x7FFFFF
_ORDERED_POS_INF = 0x7F800000


def _float_of_ordered(k):
    return lax.bitcast_convert_type(k ^ ((k >> 31) & jnp.int32(0x7FFFFFFF)), F32)


def _topk_mask(key_scr, msk_scr, nk, k_sel, idx_bits):
    cw, rows = key_scr.shape[1], key_scr.shape[2]
    pos = lax.broadcasted_iota(jnp.int32, (cw, rows), 0)

    def count(pred_fn):
        def body(c, acc):
            hit = jnp.where(pred_fn(key_scr[c], pos + c * cw), 1.0, 0.0)
            parts = [hit[g * SUBLANES:(g + 1) * SUBLANES, :] for g in range(cw // SUBLANES)]
            while len(parts) > 1:
                parts = [parts[g] + parts[g + 1] for g in range(0, len(parts) - 1, 2)] + parts[len(parts) & ~1:]
            return acc + parts[0]
        acc = lax.fori_loop(0, nk, body, jnp.zeros((SUBLANES, rows), F32))
        return jnp.sum(acc, axis=0, keepdims=True)

    sign = jnp.int32(-2 ** 31)
    max_off = jnp.int32((_ORDERED_POS_INF - _ORDERED_NEG_INF - 2 ** 31))

    def thr_bit(it, off):
        cand = off + lax.shift_left(jnp.int32(1), 31 - it)
        ok = (cand ^ sign) <= max_off
        t = _float_of_ordered(jnp.where(ok, jnp.int32(_ORDERED_NEG_INF) + cand, jnp.int32(_ORDERED_POS_INF)))
        cnt = count(lambda key, p: key >= t)
        return jnp.where(ok & (cnt >= k_sel), cand, off)

    off = lax.fori_loop(0, 32, thr_bit, jnp.zeros((1, rows), jnp.int32))
    thr = _float_of_ordered(jnp.int32(_ORDERED_NEG_INF) + off)
    need = k_sel - count(lambda key, p: key > thr)

    def lim_bit(it, lim):
        cand = lim + lax.shift_left(jnp.int32(1), idx_bits - 1 - it)
        cnt = count(lambda key, p: (key == thr) & (p < cand))
        return jnp.where(cnt < need, cand, lim)

    tied = jnp.max(count(lambda key, p: key >= thr)) > k_sel
    lim = lax.cond(tied,
                   lambda: lax.fori_loop(0, idx_bits, lim_bit, jnp.zeros((1, rows), jnp.int32)),
                   lambda: jnp.full((1, rows), (1 << idx_bits) - 1, jnp.int32))

    def write(c, carry):
        key = key_scr[c]
        sel = (key > thr) | ((key == thr) & (pos + c * cw <= lim))
        msk_scr[c] = jnp.where(sel, 0.0, NEG_INF).T
        return carry

    lax.fori_loop(0, nk, write, 0)


def _attend(units, q_of, kcols, vcols, slopes, k_ref, v_ref, msk_scr, s_scr, m_scr, l_scr, acc_scr,
            q0, nk, sq, ck):
    rel = (lax.broadcasted_iota(jnp.int32, (sq, ck), 0) - lax.broadcasted_iota(jnp.int32, (sq, ck), 1))
    kcol = lax.broadcasted_iota(jnp.int32, (1, ck), 1)
    m_scr[...] = jnp.full(m_scr.shape, NEG_INF, F32)
    l_scr[...] = jnp.zeros(l_scr.shape, F32)
    acc_scr[...] = jnp.zeros(acc_scr.shape, F32)

    def pass1(kc, masked):
        k0 = pl.multiple_of(kc * ck, ck)
        kpos = (k0 + kcol).astype(F32)
        extra = None if msk_scr is None else msk_scr[kc]
        ok = (q0 - k0) + rel >= 0 if masked else None
        for u in units:
            kb = k_ref[pl.ds(k0, ck), kcols(u)]
            s = _dot_nt(q_of(u), kb) + slopes[u] * kpos
            if extra is not None:
                s = s + extra
            if masked:
                s = jnp.where(ok, s, NEG_INF)
            s_scr[u, kc] = s
            m_scr[u] = jnp.maximum(m_scr[u], _fold_lanes(s, jnp.maximum))

    def pass1_full(kc, carry):
        pass1(kc, masked=False)
        return carry

    lax.fori_loop(0, nk - 1, pass1_full, 0)
    pass1(nk - 1, masked=True)
    m_col = [jnp.max(m_scr[u], axis=1, keepdims=True) for u in units]

    def pass2(kc, carry):
        k0 = pl.multiple_of(kc * ck, ck)
        for u in units:
            p = jnp.exp(s_scr[u, kc] - m_col[u])
            l_scr[u] = l_scr[u] + _fold_lanes(p, jnp.add)
            acc_scr[u] = acc_scr[u] + _dot(p.astype(BF16), v_ref[pl.ds(k0, ck), vcols(u)])
        return carry

    lax.fori_loop(0, nk, pass2, 0)


def _half_masks(shape):
    lane = lax.broadcasted_iota(jnp.int32, shape, 1)
    return lane < (LANES // 2), lane >= (LANES // 2)


def _diff_lambda(lv, lam_init):
    d1 = jnp.sum(lv[0:1, :] * lv[1:2, :], axis=1, keepdims=True)
    d2 = jnp.sum(lv[2:3, :] * lv[3:4, :], axis=1, keepdims=True)
    return jnp.exp(d1) - jnp.exp(d2) + lam_init


def _diff_combine(o0, o1, lam, g, lam_init):
    o = o0 - lam * o1
    o = o * lax.rsqrt(jnp.mean(o * o, axis=-1, keepdims=True) + RMS_EPS)
    return o * g * (1.0 - lam_init)


def _diff_attn_kernel(lv_ref, g_ref, q_ref, k_ref, v_ref, o_ref, s_scr, m_scr, l_scr, acc_scr,
                      *, sq, ck, lam_init):
    j = pl.program_id(1)
    q0 = j * sq
    nk = (q0 + sq - 1) // ck + 1
    lo, hi = _half_masks((sq, LANES))
    qm = []
    for h in range(DA_HEADS):
        qh = q_ref[:, h * LANES:(h + 1) * LANES]
        qm.append(jnp.where(lo, qh, jnp.zeros_like(qh)))
        qm.append(jnp.where(hi, qh, jnp.zeros_like(qh)))
    units = list(range(2 * DA_HEADS))
    cols = lambda u: slice((u // 2) * LANES, (u // 2 + 1) * LANES)
    slopes = [s for s in _alibi_slopes(DA_HEADS) for _ in range(2)]
    _attend(units, lambda u: qm[u], cols, cols, slopes, k_ref, v_ref, None, s_scr, m_scr, l_scr, acc_scr,
            q0, nk, sq, ck)
    lam = _diff_lambda(lv_ref[...], lam_init)
    for h in range(DA_HEADS):
        o0 = acc_scr[2 * h] / jnp.sum(l_scr[2 * h], axis=1, keepdims=True)
        o1 = acc_scr[2 * h + 1] / jnp.sum(l_scr[2 * h + 1], axis=1, keepdims=True)
        o = _diff_combine(o0, o1, lam, g_ref[...], lam_init)
        o_ref[:, h * LANES:(h + 1) * LANES] = o.astype(o_ref.dtype)


def _prompt_diff_attn(lv, g, qa, ka, va, lam_init):
    b, s, _ = qa.shape
    sq = min(s, 128)
    ck = min(s, 512)
    assert ck % sq == 0 and s % ck == 0
    nc = s // ck
    u = 2 * DA_HEADS
    blk_q = pl.BlockSpec((None, sq, DA_WIDTH), lambda bi, j: (bi, j, 0))
    blk_kv = pl.BlockSpec((None, s, DA_WIDTH), lambda bi, j: (bi, 0, 0))
    full = lambda a: pl.BlockSpec(a.shape, lambda bi, j: (0,) * a.ndim)
    return pl.pallas_call(
        functools.partial(_diff_attn_kernel, sq=sq, ck=ck, lam_init=lam_init),
        out_shape=jax.ShapeDtypeStruct((b, s, DA_WIDTH), BF16),
        grid=(b, s // sq),
        in_specs=[full(lv), full(g), blk_q, blk_kv, blk_kv],
        out_specs=blk_q,
        scratch_shapes=[pltpu.VMEM((u, nc, sq, ck), F32), pltpu.VMEM((u, sq, LANES), F32),
                        pltpu.VMEM((u, sq, LANES), F32), pltpu.VMEM((u, sq, LANES), F32)],
        compiler_params=_params(("parallel", "arbitrary")),
        name="prompt_diff_attn",
    )(lv, g, qa, ka, va)


def _index_scores_chunk(qim, wi_t, kblk):
    total = None
    for h in range(IDX_HEADS):
        r = jnp.maximum(_dot_nt(kblk, qim[h]), 0.0)
        term = wi_t[h:h + 1, :] * r
        total = term if total is None else total + term
    return total


def _sparse_attn_kernel(qi_ref, wi_ref, ki_ref, qs_ref, ks_ref, vs_ref, o_ref,
                        key_scr, msk_scr, s_scr, m_scr, l_scr, acc_scr, *, sq, ck, k_sel, idx_bits):
    j = pl.program_id(1)
    q0 = j * sq
    nk = (q0 + sq - 1) // ck + 1
    lo, hi = _half_masks((sq, LANES))

    def masked(ref):
        out = []
        for p in range(ref.shape[1] // LANES):
            blk = ref[:, p * LANES:(p + 1) * LANES]
            out.append(jnp.where(lo, blk, jnp.zeros_like(blk)))
            out.append(jnp.where(hi, blk, jnp.zeros_like(blk)))
        return out

    qim = masked(qi_ref)
    wi_t = wi_ref[...]
    rel = (lax.broadcasted_iota(jnp.int32, (ck, sq), 1) - lax.broadcasted_iota(jnp.int32, (ck, sq), 0))

    def scores(kc, carry):
        k0 = pl.multiple_of(kc * ck, ck)
        tot = _index_scores_chunk(qim, wi_t, ki_ref[pl.ds(k0, ck), :])
        valid = (q0 - k0) + rel >= 0
        key_scr[kc] = jnp.where(valid, tot, NEG_INF)
        return carry

    lax.fori_loop(0, nk, scores, 0)
    _topk_mask(key_scr, msk_scr, nk, k_sel, idx_bits)

    qsm = masked(qs_ref)
    units = list(range(SA_HEADS))
    cols = lambda u: slice((u // 2) * LANES, (u // 2 + 1) * LANES)
    _attend(units, lambda u: qsm[u], cols, cols, _alibi_slopes(SA_HEADS), ks_ref, vs_ref, msk_scr,
            s_scr, m_scr, l_scr, acc_scr, q0, nk, sq, ck)
    for p in range(SA_HEADS // 2):
        oa = acc_scr[2 * p] / jnp.sum(l_scr[2 * p], axis=1, keepdims=True)
        ob = acc_scr[2 * p + 1] / jnp.sum(l_scr[2 * p + 1], axis=1, keepdims=True)
        o_ref[:, p * LANES:(p + 1) * LANES] = jnp.where(lo, oa, ob).astype(o_ref.dtype)


def _prompt_sparse_attn(qi, wi_t, ki2, qs, ks, vs):
    b, s, _ = qs.shape
    sq = min(s, 256)
    ck = min(s, 512)
    assert ck % sq == 0 and s % ck == 0
    nc = s // ck
    k_sel = min(TOPK_MAX, s // 4)
    idx_bits = max(1, (s - 1).bit_length())
    blk_q = lambda w: pl.BlockSpec((None, sq, w), lambda bi, j: (bi, j, 0))
    blk_kv = lambda w: pl.BlockSpec((None, s, w), lambda bi, j: (bi, 0, 0))
    u = SA_HEADS
    return pl.pallas_call(
        functools.partial(_sparse_attn_kernel, sq=sq, ck=ck, k_sel=k_sel, idx_bits=idx_bits),
        out_shape=jax.ShapeDtypeStruct((b, s, SA_WIDTH), BF16),
        grid=(b, s // sq),
        in_specs=[blk_q(IDX_HEADS * IDX_HD), pl.BlockSpec((None, IDX_HEADS, sq), lambda bi, j: (bi, 0, j)),
                  blk_kv(LANES), blk_q(SA_WIDTH), blk_kv(SA_WIDTH), blk_kv(SA_WIDTH)],
        out_specs=blk_q(SA_WIDTH),
        scratch_shapes=[pltpu.VMEM((nc, ck, sq), F32), pltpu.VMEM((nc, sq, ck), F32),
                        pltpu.VMEM((u, nc, sq, ck), F32), pltpu.VMEM((u, sq, LANES), F32),
                        pltpu.VMEM((u, sq, LANES), F32), pltpu.VMEM((u, sq, LANES), F32)],
        compiler_params=_params(("parallel", "arbitrary")),
        name="prompt_sparse_attn",
    )(qi, wi_t, ki2, qs, ks, vs)


def _sample_index_kernel(pt_ref, qi_ref, wi_ref, kin_ref, cache_ref, o_ref, buf, sem,
                         *, layer, sb, n_pages, page):
    g = pl.program_id(0)
    ng = pl.num_programs(0)
    past = n_pages * page

    def copies(step, slot):
        out = []
        for s in range(sb):
            for p in range(n_pages):
                src = cache_ref.at[layer, pt_ref[step * sb + s, p]]
                out.append(pltpu.make_async_copy(src, buf.at[slot, s, :, pl.ds(p * page, page)], sem.at[slot]))
        return out

    slot = g % 2

    @pl.when(g == 0)
    def _():
        for c in copies(0, 0):
            c.start()

    @pl.when(g + 1 < ng)
    def _():
        for c in copies(g + 1, 1 - slot):
            c.start()

    for c in copies(g, slot):
        c.wait()

    lane = lax.broadcasted_iota(jnp.int32, (1, page), 1)
    for s in range(sb):
        q = qi_ref[s]
        w = wi_ref[s]
        r = jnp.maximum(_dot(q, buf[slot, s].astype(BF16)), 0.0)
        o_ref[s:s + 1, 0:past] = jnp.sum(w * r, axis=0, keepdims=True)
        kn = kin_ref[s].astype(BF16).astype(F32)
        rn = jnp.maximum(jnp.sum(q.astype(F32) * kn, axis=1, keepdims=True), 0.0)
        tot = jnp.sum(w * rn, axis=0, keepdims=True)
        o_ref[s:s + 1, past:past + page] = jnp.where(lane == 0, tot, NEG_INF)


def _sample_index_scores(page_table, qi8, wi8, ki_new, idx_kt, layer):
    bd, n_pages = page_table.shape
    page = idx_kt.shape[3]
    sb = min(bd, SUBLANES)
    assert bd % sb == 0
    lpad = (n_pages + 1) * page
    grid_spec = pltpu.PrefetchScalarGridSpec(
        num_scalar_prefetch=1,
        grid=(bd // sb,),
        in_specs=[
            pl.BlockSpec((sb, IDX_HEADS, IDX_HD), lambda g, pt: (g, 0, 0)),
            pl.BlockSpec((sb, IDX_HEADS, 1), lambda g, pt: (g, 0, 0)),
            pl.BlockSpec((sb, 1, IDX_HD), lambda g, pt: (g, 0, 0)),
            pl.BlockSpec(memory_space=pl.ANY),
        ],
        out_specs=pl.BlockSpec((sb, lpad), lambda g, pt: (g, 0)),
        scratch_shapes=[pltpu.VMEM((2, sb, IDX_HD, n_pages * page), F32), pltpu.SemaphoreType.DMA((2,))],
    )
    return pl.pallas_call(
        functools.partial(_sample_index_kernel, layer=layer, sb=sb, n_pages=n_pages, page=page),
        out_shape=jax.ShapeDtypeStruct((bd, lpad), F32),
        grid_spec=grid_spec,
        compiler_params=_params(("arbitrary",)),
        name="sample_index_scores",
    )(page_table, qi8, wi8, ki_new, idx_kt)


def _sample_select_kernel(sc_ref, msk_ref, key_scr, msk_scr, *, k_sel, cw, idx_bits):
    nc = key_scr.shape[0]
    for c in range(nc):
        key_scr[c] = sc_ref[:, c * cw:(c + 1) * cw].T
    _topk_mask(key_scr, msk_scr, nc, k_sel, idx_bits)
    for c in range(nc):
        msk_ref[:, c * cw:(c + 1) * cw] = msk_scr[c]


def _sample_select(scores, k_sel, cw):
    bd, lpad = scores.shape
    nc = lpad // cw
    idx_bits = max(1, (lpad - 1).bit_length())
    return pl.pallas_call(
        functools.partial(_sample_select_kernel, k_sel=k_sel, cw=cw, idx_bits=idx_bits),
        out_shape=jax.ShapeDtypeStruct((bd, lpad), F32),
        grid=(1,),
        in_specs=[pl.BlockSpec((bd, lpad), lambda i: (0, 0))],
        out_specs=pl.BlockSpec((bd, lpad), lambda i: (0, 0)),
        scratch_shapes=[pltpu.VMEM((nc, cw, bd), F32), pltpu.VMEM((nc, bd, cw), F32)],
        compiler_params=_params(("arbitrary",)),
        name="sample_select",
    )(scores)


def _online_update(s, pv_fn, m_scr, l_scr, acc_scr):
    m_old = m_scr[...]
    m_new = jnp.maximum(m_old, jnp.max(s, axis=1, keepdims=True))
    m_safe = jnp.where(m_new == NEG_INF, 0.0, m_new)
    alpha = jnp.exp(m_old - m_safe)
    p = jnp.exp(s - m_safe)
    l_scr[...] = alpha * l_scr[...] + jnp.sum(p, axis=1, keepdims=True)
    acc_scr[...] = alpha * acc_scr[...] + pv_fn(p.astype(BF16))
    m_scr[...] = m_new


def _sample_attn_kernel(pt_ref, lv_ref, g_ref, qa_ref, qs_ref, kan_ref, van_ref, ksn_ref, vsn_ref,
                        msk_ref, mskn_ref, dak_hbm, dav_hbm, sak_hbm, sav_hbm, a_ref, o_ref,
                        dak_buf, dav_buf, sak_buf, sav_buf, sem, qda_scr, qsa_scr, m_scr, l_scr, acc_scr,
                        *, layer, n_pages, pg, page, lam_init):
    b = pl.program_id(0)
    hf = pl.program_id(1)
    nh = pl.num_programs(1)
    step = b * nh + hf
    n_steps = pl.num_programs(0) * nh
    slot = step % 2
    past = n_pages * page
    span = pg * page

    def copies(bb, hh, sl):
        out = []
        for j in range(pg):
            pid = pt_ref[bb, hh * pg + j]
            cols = pl.ds(j * page, page)
            out.append(pltpu.make_async_copy(dak_hbm.at[layer, pid], dak_buf.at[sl, :, cols], sem.at[sl, 0]))
            out.append(pltpu.make_async_copy(dav_hbm.at[layer, pid],
                                             dav_buf.at[sl, pl.ds(j * page * DA_HEADS, page * DA_HEADS), :],
                                             sem.at[sl, 1]))
            out.append(pltpu.make_async_copy(sak_hbm.at[layer, pid], sak_buf.at[sl, :, cols], sem.at[sl, 2]))
            out.append(pltpu.make_async_copy(sav_hbm.at[layer, pid], sav_buf.at[sl, :, cols], sem.at[sl, 3]))
        return out

    @pl.when(step == 0)
    def _():
        for c in copies(0, 0, 0):
            c.start()

    @pl.when(step + 1 < n_steps)
    def _():
        wrap = hf + 1 == nh
        for c in copies(jnp.where(wrap, b + 1, b), jnp.where(wrap, 0, hf + 1), 1 - slot):
            c.start()

    row = lax.broadcasted_iota(jnp.int32, (SUBLANES, DA_WIDTH), 0)
    col = lax.broadcasted_iota(jnp.int32, (SUBLANES, DA_WIDTH), 1)
    row1 = lax.broadcasted_iota(jnp.int32, (SUBLANES, 1), 0)
    slope_da = jnp.zeros((SUBLANES, 1), F32)
    for h, sl in enumerate(_alibi_slopes(DA_HEADS)):
        slope_da = jnp.where(row1 // 2 == h, sl, slope_da)
    slope_sa = jnp.zeros((SUBLANES, 1), F32)
    for h, sl in enumerate(_alibi_slopes(SA_HEADS)):
        slope_sa = jnp.where(row1 == h, sl, slope_sa)

    @pl.when(hf == 0)
    def _():
        qa = jnp.broadcast_to(qa_ref[...].astype(F32), (SUBLANES, DA_WIDTH))
        qda_scr[...] = jnp.where(col // DA_HD == row, qa, 0.0)
        qs = jnp.broadcast_to(qs_ref[...].astype(F32), (SUBLANES, SA_WIDTH))
        qsa_scr[...] = jnp.where(col // SA_HD == row, qs, 0.0)
        m_scr[...] = jnp.full(m_scr.shape, NEG_INF, F32)
        l_scr[...] = jnp.zeros(l_scr.shape, F32)
        acc_scr[...] = jnp.zeros(acc_scr.shape, F32)

    for c in copies(b, hf, slot):
        c.wait()

    kpos = hf * span + lax.broadcasted_iota(jnp.int32, (1, span), 1)
    dist = (past - kpos).astype(F32)

    def pv_da(p):
        return jnp.concatenate(
            [_dot(p, dav_buf[slot, pl.ds(h, span, stride=DA_HEADS), :].astype(BF16)) for h in range(DA_HEADS)],
            axis=1)

    s = _dot(qda_scr[...].astype(BF16), dak_buf[slot].astype(BF16)) - slope_da * dist
    _online_update(s, pv_da, m_scr.at[0], l_scr.at[0], acc_scr.at[0])
    s = _dot(qsa_scr[...].astype(BF16), sak_buf[slot].astype(BF16)) - slope_sa * dist + msk_ref[...]
    _online_update(s, lambda p: _dot_nt(p, sav_buf[slot].astype(BF16)), m_scr.at[1], l_scr.at[1], acc_scr.at[1])

    @pl.when(hf == nh - 1)
    def _():
        rnd = lambda ref: ref[...].astype(BF16).astype(F32)
        s = jnp.sum(qda_scr[...] * rnd(kan_ref), axis=1, keepdims=True)
        _online_update(s, lambda p: p.astype(F32) * rnd(van_ref), m_scr.at[0], l_scr.at[0], acc_scr.at[0])
        s = jnp.sum(qsa_scr[...] * rnd(ksn_ref), axis=1, keepdims=True) + mskn_ref[:, 0:1]
        _online_update(s, lambda p: p.astype(F32) * rnd(vsn_ref), m_scr.at[1], l_scr.at[1], acc_scr.at[1])

        lam = _diff_lambda(lv_ref[...], lam_init)
        oda = acc_scr[0] / l_scr[0]
        for h in range(DA_HEADS):
            cs = slice(h * LANES, (h + 1) * LANES)
            o = _diff_combine(oda[2 * h:2 * h + 1, cs], oda[2 * h + 1:2 * h + 2, cs], lam, g_ref[...], lam_init)
            a_ref[:, cs] = o.astype(a_ref.dtype)
        osa = acc_scr[1] / l_scr[1]
        lo, _ = _half_masks((1, LANES))
        for pr in range(SA_HEADS // 2):
            cs = slice(pr * LANES, (pr + 1) * LANES)
            o_ref[:, cs] = jnp.where(lo, osa[2 * pr:2 * pr + 1, cs], osa[2 * pr + 1:2 * pr + 2, cs]).astype(o_ref.dtype)


def _sample_attn(page_table, lv, g, qa, qs, ka_new, va_new, ks_new, vs_new, msk, da_kt, da_v, sa_kt, sa_vt,
                 layer, lam_init):
    bd, n_pages = page_table.shape
    page = da_kt.shape[3]
    pg = math.gcd(n_pages, 8)
    nh = n_pages // pg
    span = pg * page
    tok = lambda w: pl.BlockSpec((None, 1, w), lambda b, hf, pt: (b, 0, 0))
    full = lambda a: pl.BlockSpec(a.shape, lambda b, hf, pt: (0,) * a.ndim)
    hbm = pl.BlockSpec(memory_space=pl.ANY)
    grid_spec = pltpu.PrefetchScalarGridSpec(
        num_scalar_prefetch=1,
        grid=(bd, nh),
        in_specs=[full(lv), full(g), tok(DA_WIDTH), tok(SA_WIDTH), tok(DA_WIDTH), tok(DA_WIDTH),
                  tok(SA_WIDTH), tok(SA_WIDTH),
                  pl.BlockSpec((None, 1, span), lambda b, hf, pt: (b, 0, hf)),
                  pl.BlockSpec((None, 1, page), lambda b, hf, pt: (b, 0, n_pages)),
                  hbm, hbm, hbm, hbm],
        out_specs=[tok(DA_WIDTH), tok(SA_WIDTH)],
        scratch_shapes=[pltpu.VMEM((2, DA_WIDTH, span), F32), pltpu.VMEM((2, span * DA_HEADS, LANES), F32),
                        pltpu.VMEM((2, SA_WIDTH, span), F32), pltpu.VMEM((2, SA_WIDTH, span), F32),
                        pltpu.SemaphoreType.DMA((2, 4)),
                        pltpu.VMEM((SUBLANES, DA_WIDTH), F32), pltpu.VMEM((SUBLANES, SA_WIDTH), F32),
                        pltpu.VMEM((2, SUBLANES, 1), F32), pltpu.VMEM((2, SUBLANES, 1), F32),
                        pltpu.VMEM((2, SUBLANES, DA_WIDTH), F32)],
    )
    return pl.pallas_call(
        functools.partial(_sample_attn_kernel, layer=layer, n_pages=n_pages, pg=pg, page=page, lam_init=lam_init),
        out_shape=[jax.ShapeDtypeStruct((bd, 1, DA_WIDTH), BF16), jax.ShapeDtypeStruct((bd, 1, SA_WIDTH), BF16)],
        grid_spec=grid_spec,
        compiler_params=_params(("arbitrary", "arbitrary")),
        name="sample_attn",
    )(page_table, lv, g, qa, qs, ka_new, va_new, ks_new, vs_new, msk, msk, da_kt, da_v, sa_kt, sa_vt)


def _merge_kernel(a_ref, o_ref, ga_ref, gb_ref, x_ref, wa_ref, wb_ref, wo_ref, g_ref, b_ref, wq_ref,
                  h_ref, q_ref, *, alpha):
    merged = (jax.nn.sigmoid(ga_ref[...]) * _dot(a_ref[...], wa_ref[...])
              + jax.nn.sigmoid(gb_ref[...]) * _dot(o_ref[...], wb_ref[...]))
    h = _layer_norm(alpha * x_ref[...] + _dot(merged.astype(BF16), wo_ref[...]), g_ref[...], b_ref[...])
    h_ref[...] = h
    q = _dot(h.astype(BF16), wq_ref[...]).astype(q_ref.dtype)
    for hd in range(PEER_HEADS):
        q_ref[hd] = q[:, hd * PEER_QD:(hd + 1) * PEER_QD]


def _merge(a, o, ga, gb, x, wa, wb, wo, g, b, wq, alpha):
    t, d_model = x.shape
    tm = min(t, 256)
    row = lambda n: pl.BlockSpec((tm, n), lambda i: (i, 0))
    full = lambda arr: pl.BlockSpec(arr.shape, lambda i: (0,) * arr.ndim)
    return pl.pallas_call(
        functools.partial(_merge_kernel, alpha=alpha),
        out_shape=[jax.ShapeDtypeStruct((t, d_model), F32), jax.ShapeDtypeStruct((PEER_HEADS, t, PEER_QD), BF16)],
        grid=(t // tm,),
        in_specs=[row(DA_WIDTH), row(SA_WIDTH), row(d_model), row(d_model), row(d_model),
                  full(wa), full(wb), full(wo), full(g), full(b), full(wq)],
        out_specs=[row(d_model), pl.BlockSpec((PEER_HEADS, tm, PEER_QD), lambda i: (0, i, 0))],
        compiler_params=_params(("parallel",)),
        name="merge_ln_peerq",
    )(a, o, ga, gb, x, wa, wb, wo, g, b, wq)


def _take_max(s, ridx, exact):
    m = jnp.max(s, axis=0, keepdims=True)
    if not exact:
        return m, s == m
    first = jnp.min(jnp.where(s == m, ridx, float(s.shape[0])), axis=0, keepdims=True)
    return m, ridx == first


def _count_mismatch(flag_arrays, want):
    off = [jnp.abs(jnp.sum(f, axis=0, keepdims=True) - float(want)) for f in flag_arrays]
    return jnp.max(functools.reduce(jnp.add, off)) > 0.0


_CAND_LIM = [PEER_TOPK // (a + 1) for a in range(SUBLANES)]


def _top_subkeys(s0, s1, r0_scr, r1_scr, v_scr, exact):
    topk = PEER_TOPK
    rows, t = s0.shape
    ridx = lax.broadcasted_iota(jnp.int32, (rows, t), 0).astype(F32)
    kidx = lax.broadcasted_iota(jnp.int32, (topk, t), 0)
    r0_scr[...] = jnp.full((rows, t), float(topk), F32)
    r1_scr[...] = jnp.full((rows, t), float(topk), F32)

    def body(k, carry):
        c0, c1, v0, v1 = carry
        kf = jnp.asarray(k).astype(F32)
        m0, hit0 = _take_max(c0, ridx, exact)
        m1, hit1 = _take_max(c1, ridx, exact)
        r0_scr[...] = jnp.where(hit0, kf, r0_scr[...])
        r1_scr[...] = jnp.where(hit1, kf, r1_scr[...])
        return (jnp.where(hit0, NEG_INF, c0), jnp.where(hit1, NEG_INF, c1),
                jnp.where(kidx == k, m0, v0), jnp.where(kidx == k, m1, v1))

    zero = jnp.zeros((topk, t), F32)
    _, _, v0, v1 = lax.fori_loop(0, topk, body, (s0, s1, zero, zero))
    v_scr[0] = v0
    v_scr[1] = v1


def _top_candidates(cand, taken_scr, tv_scr, exact):
    topk = PEER_TOPK
    crows, t = cand.shape
    cidx = lax.broadcasted_iota(jnp.int32, (crows, t), 0).astype(F32)
    kidx = lax.broadcasted_iota(jnp.int32, (topk, t), 0)

    def body(k, carry):
        c, taken, tv = carry
        m, hit = _take_max(c, cidx, exact)
        return jnp.where(hit, NEG_INF, c), jnp.where(hit, 1.0, taken), jnp.where(kidx == k, m, tv)

    _, taken, tv = lax.fori_loop(0, topk, body, (cand, jnp.zeros((crows, t), F32), jnp.zeros((topk, t), F32)))
    taken_scr[...] = taken
    tv_scr[...] = tv


def _peer_route_head(q0, q1, sk0, sk1, r0_scr, r1_scr, v_scr, taken_scr, tv_scr):
    topk = PEER_TOPK
    s0 = _dot_nt(sk0, q0)
    s1 = _dot_nt(sk1, q1)
    rows, t = s0.shape

    sub = lax.broadcasted_iota(jnp.int32, (SUBLANES, t), 0)

    def select(exact):
        _top_subkeys(s0, s1, r0_scr, r1_scr, v_scr, exact)
        v0 = v_scr[0]
        v1 = v_scr[1]
        groups = [v0[0:1, :] + v1]
        for a in range(1, SUBLANES):
            groups.append(jnp.where(sub < _CAND_LIM[a], v0[a:a + 1, :] + v1[0:SUBLANES, :], NEG_INF))
        groups.append(v0[SUBLANES:topk, :] + v1[0:1, :])
        cand = jnp.concatenate(groups, axis=0)
        _top_candidates(cand, taken_scr, tv_scr, exact)

    select(exact=False)
    in_top = lambda r_scr: jnp.where(r_scr[...] < float(topk), 1.0, 0.0)

    @pl.when(_count_mismatch([in_top(r0_scr), in_top(r1_scr), taken_scr[...]], topk))
    def _():
        select(exact=True)

    v0 = v_scr[0]
    v1 = v_scr[1]
    taken = taken_scr[...]
    tv = tv_scr[...]
    z = jnp.sum(jnp.exp(tv - tv[0:1, :]), axis=0, keepdims=True)

    r0 = r0_scr[...]
    r1 = r1_scr[...]
    n_sel = jnp.zeros((rows, t), F32)
    for a in range(topk):
        if a == 0:
            n_a = jnp.sum(taken[0:topk, :], axis=0, keepdims=True)
        elif a < SUBLANES:
            base = topk + SUBLANES * (a - 1)
            n_a = jnp.sum(taken[base:base + SUBLANES, :], axis=0, keepdims=True)
        else:
            base = topk + SUBLANES * (SUBLANES - 1) + (a - SUBLANES)
            n_a = taken[base:base + 1, :]
        n_sel = jnp.where(r0 == float(a), n_a, n_sel)
    a_out = jnp.where(r0 < float(topk), jnp.exp(s0 - v0[0:1, :]) / z, 0.0)
    b_out = jnp.where(r1 < float(topk), jnp.exp(s1 - v1[0:1, :]), 0.0)
    return a_out, n_sel, b_out, r1


def _peer_route_kernel(q_ref, sk_ref, a_ref, n_ref, b_ref, r1_ref, r0_scr, r1_scr, v_scr, taken_scr, tv_scr):
    half = PEER_QD // 2

    def head(h, carry):
        a, n, b, r1 = _peer_route_head(q_ref[h, :, :half], q_ref[h, :, half:], sk_ref[h, 0], sk_ref[h, 1],
                                       r0_scr, r1_scr, v_scr, taken_scr, tv_scr)
        a_ref[h] = a
        n_ref[h] = n
        b_ref[h] = b.astype(b_ref.dtype)
        r1_ref[h] = r1.astype(r1_ref.dtype)
        return carry

    lax.fori_loop(0, PEER_HEADS, head, 0)


def _peer_route(q, sub_keys):
    t = q.shape[1]
    tt = min(t, LANES)
    blk = pl.BlockSpec((PEER_HEADS, PEER_NKEYS, tt), lambda i: (0, 0, i))
    shp = lambda dt: jax.ShapeDtypeStruct((PEER_HEADS, PEER_NKEYS, t), dt)
    return pl.pallas_call(
        _peer_route_kernel,
        out_shape=[shp(F32), shp(F32), shp(BF16), shp(BF16)],
        grid=(t // tt,),
        in_specs=[pl.BlockSpec((PEER_HEADS, tt, PEER_QD), lambda i: (0, i, 0)),
                  pl.BlockSpec(sub_keys.shape, lambda i: (0, 0, 0, 0))],
        out_specs=[blk] * 4,
        scratch_shapes=[pltpu.VMEM((PEER_NKEYS, tt), F32), pltpu.VMEM((PEER_NKEYS, tt), F32),
                        pltpu.VMEM((2, PEER_TOPK, tt), F32),
                        pltpu.VMEM((PEER_TOPK + SUBLANES * SUBLANES, tt), F32), pltpu.VMEM((PEER_TOPK, tt), F32)],
        compiler_params=_params(("parallel",)),
        name="peer_route",
    )(q, sub_keys)


def _gelu(x):
    return 0.5 * x * (1.0 + lax.erf(x * (2.0 ** -0.5)))


def _peer_expert_kernel(h_ref, u_ref, vt_ref, a_ref, n_ref, b_ref, r1_ref, g_ref, beta_ref, y_ref,
                        hb_scr, pre_scr, act_scr, acc_scr, *, ci, alpha):
    c = pl.program_id(1)

    @pl.when(c == 0)
    def _():
        hb_scr[...] = h_ref[...].astype(BF16)
        acc_scr[...] = jnp.zeros(acc_scr.shape, F32)

    pre_scr[...] = _dot_nt(u_ref[...], hb_scr[...])
    for i in range(ci):
        rows = slice(i * PEER_NKEYS, (i + 1) * PEER_NKEYS)
        w = None
        for hd in range(PEER_HEADS):
            a_row = a_ref[hd, i:i + 1, :].astype(BF16)
            n_row = n_ref[hd, i:i + 1, :].astype(BF16)
            b = b_ref[hd]
            gate = a_row * jnp.where(r1_ref[hd] < n_row, b, jnp.zeros_like(b))
            w = gate if w is None else w + gate
        act_scr[rows, :] = _gelu(pre_scr[rows, :]).astype(BF16) * w
    acc_scr[...] = acc_scr[...] + _dot(vt_ref[...], act_scr[...])

    @pl.when(c == pl.num_programs(1) - 1)
    def _():
        hv = h_ref[...]
        y_ref[...] = _layer_norm(alpha * hv + acc_scr[...].T, g_ref[...], beta_ref[...])


def _peer_experts(h, u_tab, vt_tab, ra, rn, rb, rr1, g, beta, alpha):
    t, d_model = h.shape
    tt = min(t, 512)
    ci = SUBLANES
    ne = ci * PEER_NKEYS
    nch = u_tab.shape[0] // ne
    sml = pl.BlockSpec((PEER_HEADS, ci, tt), lambda i, c: (0, c, i))
    big = pl.BlockSpec((PEER_HEADS, PEER_NKEYS, tt), lambda i, c: (0, 0, i))
    full = lambda arr: pl.BlockSpec(arr.shape, lambda i, c: (0,) * arr.ndim)
    return pl.pallas_call(
        functools.partial(_peer_expert_kernel, ci=ci, alpha=alpha),
        out_shape=jax.ShapeDtypeStruct((t, d_model), F32),
        grid=(t // tt, nch),
        in_specs=[pl.BlockSpec((tt, d_model), lambda i, c: (i, 0)),
                  pl.BlockSpec((ne, d_model), lambda i, c: (c, 0)),
                  pl.BlockSpec((d_model, ne), lambda i, c: (0, c)),
                  sml, sml, big, big, full(g), full(beta)],
        out_specs=pl.BlockSpec((tt, d_model), lambda i, c: (i, 0)),
        scratch_shapes=[pltpu.VMEM((tt, d_model), BF16), pltpu.VMEM((ne, tt), F32),
                        pltpu.VMEM((ne, tt), BF16), pltpu.VMEM((d_model, tt), F32)],
        compiler_params=_params(("parallel", "arbitrary")),
        name="peer_experts",
    )(h, u_tab, vt_tab, ra, rn, rb, rr1, g, beta)


def _layer_tail(x, a, o, ga, gb, tail, alpha):
    wa, wb, wo, ln1_g, ln1_b, wq, sub_keys, u_tab, vt_tab, ln2_g, ln2_b = tail
    h, q = _merge(a, o, ga, gb, x, wa, wb, wo, ln1_g, ln1_b, wq, alpha)
    ra, rn, rb, rr1 = _peer_route(q, sub_keys)
    return _peer_experts(h, u_tab, vt_tab, ra, rn, rb, rr1, ln2_g, ln2_b, alpha)


def kernel(x_prompt, x_sample, cache_da_k, cache_da_v, cache_sa_k, cache_sa_v, cache_idx_k, page_table, w_in, lambda_q1, lambda_k1, lambda_q2, lambda_k2, da_subln_g, w_a_up, w_b_up, w_out, ln1_g, ln1_b, peer_wq, peer_sub_keys, peer_u, peer_v, ln2_g, ln2_b):
    depth = w_in.shape[0]
    bp, sp, d_model = x_prompt.shape
    bd, sd, _ = x_sample.shape
    assert sd == 1, "the sample pass handles one new token per sequence"
    n_pool, page = cache_da_k.shape[1], cache_da_k.shape[2]
    n_pages = page_table.shape[1]
    alpha = (2.0 * depth) ** 0.25
    row2 = lambda v: v.reshape(1, -1)

    da_kt = cache_da_k.transpose(0, 1, 3, 4, 5, 2).reshape(depth, n_pool, DA_WIDTH, page)
    sa_kt = cache_sa_k.transpose(0, 1, 3, 4, 2).reshape(depth, n_pool, SA_WIDTH, page)
    sa_vt = cache_sa_v.transpose(0, 1, 3, 4, 2).reshape(depth, n_pool, SA_WIDTH, page)
    idx_kt = cache_idx_k.transpose(0, 1, 3, 2)
    da_v = cache_da_v.reshape(depth, n_pool, page * DA_HEADS, 2 * DA_HD)

    y_p = x_prompt.reshape(bp * sp, d_model)
    y_s = x_sample.reshape(bd, d_model)
    p_new, s_new = [], []
    for l in range(depth):
        lam_init = 0.8 - 0.6 * math.exp(-0.3 * l)
        lv = jnp.stack([lambda_q1[l], lambda_k1[l], lambda_q2[l], lambda_k2[l]])
        g_sub = row2(da_subln_g[l])
        w_arr = _arrange_w_in(w_in[l], d_model)
        tail = (w_a_up[l].astype(BF16), w_b_up[l].astype(BF16), w_out[l].astype(BF16), row2(ln1_g[l]),
                row2(ln1_b[l]), peer_wq[l].astype(BF16), peer_sub_keys[l].astype(BF16),
                peer_u[l].astype(BF16), peer_v[l].T.astype(BF16), row2(ln2_g[l]), row2(ln2_b[l]))

        (qa, kat, kab, vaf, vab, qs, kst, ksb, vst, vsb, qi, kit, ki2, wi, ga, gb) = _in_proj(y_p, w_arr, seq=sp)
        r3 = lambda v: v.reshape(bp, sp, v.shape[-1])
        a = _prompt_diff_attn(lv, g_sub, r3(qa), r3(kab), r3(vab), lam_init)
        wi_t = wi[:, :IDX_HEADS].reshape(bp, sp, IDX_HEADS).transpose(0, 2, 1)
        o = _prompt_sparse_attn(r3(qi), wi_t, r3(ki2), r3(qs), r3(ksb), r3(vsb))
        p_new.append((kat.reshape(bp, DA_HEADS, 2, DA_HD, sp).transpose(0, 4, 1, 2, 3),
                      vaf.reshape(bp, sp, DA_HEADS, 2 * DA_HD),
                      kst.reshape(bp, SA_HEADS, SA_HD, sp).transpose(0, 3, 1, 2),
                      vst.reshape(bp, SA_HEADS, SA_HD, sp).transpose(0, 3, 1, 2),
                      kit.transpose(0, 2, 1)))
        y_p = _layer_tail(y_p, a.reshape(bp * sp, DA_WIDTH), o.reshape(bp * sp, SA_WIDTH), ga, gb, tail, alpha)

        (qa, kaf, kab, vaf, vab, qs, ksf, ksb, vsf, vsb, qi, kif, ki2, wi, ga, gb) = _in_proj(y_s, w_arr)
        t3 = lambda v: v.reshape(bd, 1, v.shape[-1])
        scores = _sample_index_scores(page_table, qi.reshape(bd, IDX_HEADS, IDX_HD),
                                      wi[:, :IDX_HEADS].reshape(bd, IDX_HEADS, 1), t3(kif), idx_kt, l)
        k_sel = min(TOPK_MAX, (n_pages * page + 1) // 4)
        msk = _sample_select(scores, k_sel, page)
        a, o = _sample_attn(page_table, lv, g_sub, t3(qa), t3(qs), t3(kaf), t3(vaf), t3(ksf), t3(vsf),
                            msk.reshape(bd, 1, msk.shape[1]), da_kt, da_v, sa_kt, sa_vt, l, lam_init)
        s_new.append((kaf.reshape(bd, 1, DA_HEADS, 2, DA_HD), vaf.reshape(bd, 1, DA_HEADS, 2 * DA_HD),
                      ksf.reshape(bd, 1, SA_HEADS, SA_HD), vsf.reshape(bd, 1, SA_HEADS, SA_HD),
                      kif.reshape(bd, 1, IDX_HD)))
        y_s = _layer_tail(y_s, a.reshape(bd, DA_WIDTH), o.reshape(bd, SA_WIDTH), ga, gb, tail, alpha)

    stk = lambda lst, j: jnp.stack([e[j] for e in lst])
    return (y_p.reshape(bp, sp, d_model), y_s.reshape(bd, 1, d_model),
            stk(p_new, 0), stk(p_new, 1), stk(p_new, 2), stk(p_new, 3), stk(p_new, 4),
            stk(s_new, 0), stk(s_new, 1), stk(s_new, 2), stk(s_new, 3), stk(s_new, 4))
```

```python
import functools
import math

import jax
import jax.numpy as jnp
from jax import lax
from jax.experimental import pallas as pl
from jax.experimental.pallas import tpu as pltpu

F32 = jnp.float32
BF16 = jnp.bfloat16
NEG_INF = float("-inf")

LANES = 128
SUBLANES = 8
DA_HEADS = 4
DA_HD = 64
DA_WIDTH = DA_HEADS * 2 * DA_HD
SA_HEADS = 8
SA_HD = 64
SA_WIDTH = SA_HEADS * SA_HD
IDX_HEADS = 8
IDX_HD = 64
TOPK_MAX = 256
PEER_HEADS = 8
PEER_NKEYS = 128
PEER_QD = 256
PEER_TOPK = 16
LN_EPS = 1e-5
RMS_EPS = 1e-5
VMEM_LIMIT = 56 * 1024 * 1024


def _alibi_slopes(n):
    return [2.0 ** (-8.0 * (i + 1) / n) for i in range(n)]


def _dot(a, b):
    return jnp.dot(a, b, preferred_element_type=F32)


def _dot_nt(a, b):
    return lax.dot_general(a, b, (((1,), (1,)), ((), ())), preferred_element_type=F32)


def _fold_lanes(x, op):
    parts = [x[:, t * LANES:(t + 1) * LANES] for t in range(x.shape[1] // LANES)]
    return functools.reduce(op, parts)


def _params(sem):
    return pltpu.CompilerParams(dimension_semantics=sem, vmem_limit_bytes=VMEM_LIMIT)


def _layer_norm(x, g, b):
    mu = jnp.mean(x, axis=-1, keepdims=True)
    xc = x - mu
    var = jnp.mean(xc * xc, axis=-1, keepdims=True)
    return xc * lax.rsqrt(var + LN_EPS) * g + b


_SEG = dict(qa=0, ka=512, va=1024, qs=1536, ks=2048, vs=2560, qi=3072, ki=3584, wi=3712, ga=3840)


def _arrange_w_in(w_in, d_model):
    offs = [0]
    for n in (DA_WIDTH, DA_WIDTH, DA_WIDTH, SA_WIDTH, SA_WIDTH, SA_WIDTH,
              IDX_HEADS * IDX_HD, IDX_HD, IDX_HEADS, d_model, d_model):
        offs.append(offs[-1] + n)
    ki = w_in[:, offs[7]:offs[8]]
    wi = w_in[:, offs[8]:offs[9]]
    pad = jnp.zeros((w_in.shape[0], LANES - IDX_HEADS), w_in.dtype)
    return jnp.concatenate([w_in[:, :offs[7]], ki, ki, wi, pad, w_in[:, offs[9]:]], axis=1).astype(BF16)


def _in_proj_kernel(x_ref, w_ref, qa_o, kaf_o, kab_o, vaf_o, vab_o, qs_o, ksf_o, ksb_o, vsf_o, vsb_o,
                    qi_o, kif_o, ki2_o, wi_o, ga_o, gb_o, *, d_model, transposed):
    xb = x_ref[...].astype(BF16)
    cache = (lambda z: z.T) if transposed else (lambda z: z)

    def seg(name, n):
        a = _SEG[name]
        return _dot(xb, w_ref[:, a:a + n])

    qa_o[...] = (seg("qa", DA_WIDTH) * (DA_HD ** -0.5)).astype(BF16)
    z = seg("ka", DA_WIDTH)
    kaf_o[...] = cache(z)
    kab_o[...] = z.astype(BF16)
    z = seg("va", DA_WIDTH)
    vaf_o[...] = z
    vab_o[...] = z.astype(BF16)
    qs_o[...] = (seg("qs", SA_WIDTH) * (SA_HD ** -0.5)).astype(BF16)
    z = seg("ks", SA_WIDTH)
    ksf_o[...] = cache(z)
    ksb_o[...] = z.astype(BF16)
    z = seg("vs", SA_WIDTH)
    vsf_o[...] = cache(z)
    vsb_o[...] = z.astype(BF16)
    qi_o[...] = (seg("qi", IDX_HEADS * IDX_HD) * (IDX_HD ** -0.5)).astype(BF16)
    z = seg("ki", 2 * IDX_HD)
    kif_o[...] = z.T[:IDX_HD, :] if transposed else z[:, :IDX_HD]
    ki2_o[...] = z.astype(BF16)
    wi_o[...] = seg("wi", LANES) * (IDX_HEADS ** -0.5)
    ga_o[...] = seg("ga", d_model)
    gb_o[...] = _dot(xb, w_ref[:, _SEG["ga"] + d_model:_SEG["ga"] + 2 * d_model])


def _in_proj(x, w_arr, seq=None):
    t, d_model = x.shape
    tm = min(t, 256)
    assert t % tm == 0
    row = lambda n: pl.BlockSpec((tm, n), lambda i: (i, 0))
    widths = [(512, BF16), (512, F32), (512, BF16), (512, F32), (512, BF16), (512, BF16), (512, F32),
              (512, BF16), (512, F32), (512, BF16), (512, BF16), (IDX_HD, F32), (LANES, BF16), (LANES, F32),
              (d_model, F32), (d_model, F32)]
    shapes = [jax.ShapeDtypeStruct((t, n), dt) for n, dt in widths]
    specs = [row(n) for n, _ in widths]
    if seq is not None:
        assert seq % tm == 0
        per = seq // tm
        for j in (1, 6, 8, 11):
            n = widths[j][0]
            shapes[j] = jax.ShapeDtypeStruct((t // seq, n, seq), F32)
            specs[j] = pl.BlockSpec((None, n, tm), lambda i: (i // per, 0, i % per))
    return pl.pallas_call(
        functools.partial(_in_proj_kernel, d_model=d_model, transposed=seq is not None),
        out_shape=shapes,
        grid=(t // tm,),
        in_specs=[row(d_model), pl.BlockSpec(w_arr.shape, lambda i: (0, 0))],
        out_specs=specs,
        compiler_params=_params(("parallel",)),
        name="in_proj",
    )(x, w_arr)


_ORDERED_NEG_INF = -(2 ** 31) + 0x7FFFFF
_ORDERED_POS_INF = 0x7F800000


def _float_of_ordered(k):
    return lax.bitcast_convert_type(k ^ ((k >> 31) & jnp.int32(0x7FFFFFFF)), F32)


def _topk_mask(key_scr, msk_scr, nk, k_sel, idx_bits):
    cw, rows = key_scr.shape[1], key_scr.shape[2]
    pos = lax.broadcasted_iota(jnp.int32, (cw, rows), 0)

    def count(pred_fn):
        def body(c, acc):
            hit = jnp.where(pred_fn(key_scr[c], pos + c * cw), 1.0, 0.0)
            parts = [hit[g * SUBLANES:(g + 1) * SUBLANES, :] for g in range(cw // SUBLANES)]
            while len(parts) > 1:
                parts = [parts[g] + parts[g + 1] for g in range(0, len(parts) - 1, 2)] + parts[len(parts) & ~1:]
            return acc + parts[0]
        acc = lax.fori_loop(0, nk, body, jnp.zeros((SUBLANES, rows), F32))
        return jnp.sum(acc, axis=0, keepdims=True)

    sign = jnp.int32(-2 ** 31)
    max_off = jnp.int32((_ORDERED_POS_INF - _ORDERED_NEG_INF - 2 ** 31))

    def thr_bit(it, off):
        cand = off + lax.shift_left(jnp.int32(1), 31 - it)
        ok = (cand ^ sign) <= max_off
        t = _float_of_ordered(jnp.where(ok, jnp.int32(_ORDERED_NEG_INF) + cand, jnp.int32(_ORDERED_POS_INF)))
        cnt = count(lambda key, p: key >= t)
        return jnp.where(ok & (cnt >= k_sel), cand, off)

    off = lax.fori_loop(0, 32, thr_bit, jnp.zeros((1, rows), jnp.int32))
    thr = _float_of_ordered(jnp.int32(_ORDERED_NEG_INF) + off)
    need = k_sel - count(lambda key, p: key > thr)

    def lim_bit(it, lim):
        cand = lim + lax.shift_left(jnp.int32(1), idx_bits - 1 - it)
        cnt = count(lambda key, p: (key == thr) & (p < cand))
        return jnp.where(cnt < need, cand, lim)

    tied = jnp.max(count(lambda key, p: key >= thr)) > k_sel
    lim = lax.cond(tied,
                   lambda: lax.fori_loop(0, idx_bits, lim_bit, jnp.zeros((1, rows), jnp.int32)),
                   lambda: jnp.full((1, rows), (1 << idx_bits) - 1, jnp.int32))

    def write(c, carry):
        key = key_scr[c]
        sel = (key > thr) | ((key == thr) & (pos + c * cw <= lim))
        msk_scr[c] = jnp.where(sel, 0.0, NEG_INF).T
        return carry

    lax.fori_loop(0, nk, write, 0)


def _attend(units, q_of, kcols, vcols, slopes, k_ref, v_ref, msk_scr, s_scr, m_scr, l_scr, acc_scr,
            q0, nk, sq, ck):
    rel = (lax.broadcasted_iota(jnp.int32, (sq, ck), 0) - lax.broadcasted_iota(jnp.int32, (sq, ck), 1))
    kcol = lax.broadcasted_iota(jnp.int32, (1, ck), 1)
    m_scr[...] = jnp.full(m_scr.shape, NEG_INF, F32)
    l_scr[...] = jnp.zeros(l_scr.shape, F32)
    acc_scr[...] = jnp.zeros(acc_scr.shape, F32)

    def pass1(kc, masked):
        k0 = pl.multiple_of(kc * ck, ck)
        kpos = (k0 + kcol).astype(F32)
        extra = None if msk_scr is None else msk_scr[kc]
        ok = (q0 - k0) + rel >= 0 if masked else None
        for u in units:
            kb = k_ref[pl.ds(k0, ck), kcols(u)]
            s = _dot_nt(q_of(u), kb) + slopes[u] * kpos
            if extra is not None:
                s = s + extra
            if masked:
                s = jnp.where(ok, s, NEG_INF)
            s_scr[u, kc] = s
            m_scr[u] = jnp.maximum(m_scr[u], _fold_lanes(s, jnp.maximum))

    def pass1_full(kc, carry):
        pass1(kc, masked=False)
        return carry

    lax.fori_loop(0, nk - 1, pass1_full, 0)
    pass1(nk - 1, masked=True)
    m_col = [jnp.max(m_scr[u], axis=1, keepdims=True) for u in units]

    def pass2(kc, carry):
        k0 = pl.multiple_of(kc * ck, ck)
        for u in units:
            p = jnp.exp(s_scr[u, kc] - m_col[u])
            l_scr[u] = l_scr[u] + _fold_lanes(p, jnp.add)
            acc_scr[u] = acc_scr[u] + _dot(p.astype(BF16), v_ref[pl.ds(k0, ck), vcols(u)])
        return carry

    lax.fori_loop(0, nk, pass2, 0)


def _half_masks(shape):
    lane = lax.broadcasted_iota(jnp.int32, shape, 1)
    return lane < (LANES // 2), lane >= (LANES // 2)


def _diff_lambda(lv, lam_init):
    d1 = jnp.sum(lv[0:1, :] * lv[1:2, :], axis=1, keepdims=True)
    d2 = jnp.sum(lv[2:3, :] * lv[3:4, :], axis=1, keepdims=True)
    return jnp.exp(d1) - jnp.exp(d2) + lam_init


def _diff_combine(o0, o1, lam, g, lam_init):
    o = o0 - lam * o1
    o = o * lax.rsqrt(jnp.mean(o * o, axis=-1, keepdims=True) + RMS_EPS)
    return o * g * (1.0 - lam_init)


def _diff_attn_kernel(lv_ref, g_ref, q_ref, k_ref, v_ref, o_ref, s_scr, m_scr, l_scr, acc_scr,
                      *, sq, ck, lam_init):
    j = pl.program_id(1)
    q0 = j * sq
    nk = (q0 + sq - 1) // ck + 1
    lo, hi = _half_masks((sq, LANES))
    qm = []
    for h in range(DA_HEADS):
        qh = q_ref[:, h * LANES:(h + 1) * LANES]
        qm.append(jnp.where(lo, qh, jnp.zeros_like(qh)))
        qm.append(jnp.where(hi, qh, jnp.zeros_like(qh)))
    units = list(range(2 * DA_HEADS))
    cols = lambda u: slice((u // 2) * LANES, (u // 2 + 1) * LANES)
    slopes = [s for s in _alibi_slopes(DA_HEADS) for _ in range(2)]
    _attend(units, lambda u: qm[u], cols, cols, slopes, k_ref, v_ref, None, s_scr, m_scr, l_scr, acc_scr,
            q0, nk, sq, ck)
    lam = _diff_lambda(lv_ref[...], lam_init)
    for h in range(DA_HEADS):
        o0 = acc_scr[2 * h] / jnp.sum(l_scr[2 * h], axis=1, keepdims=True)
        o1 = acc_scr[2 * h + 1] / jnp.sum(l_scr[2 * h + 1], axis=1, keepdims=True)
        o = _diff_combine(o0, o1, lam, g_ref[...], lam_init)
        o_ref[:, h * LANES:(h + 1) * LANES] = o.astype(o_ref.dtype)


def _prompt_diff_attn(lv, g, qa, ka, va, lam_init):
    b, s, _ = qa.shape
    sq = min(s, 128)
    ck = min(s, 512)
    assert ck % sq == 0 and s % ck == 0
    nc = s // ck
    u = 2 * DA_HEADS
    blk_q = pl.BlockSpec((None, sq, DA_WIDTH), lambda bi, j: (bi, j, 0))
    blk_kv = pl.BlockSpec((None, s, DA_WIDTH), lambda bi, j: (bi, 0, 0))
    full = lambda a: pl.BlockSpec(a.shape, lambda bi, j: (0,) * a.ndim)
    return pl.pallas_call(
        functools.partial(_diff_attn_kernel, sq=sq, ck=ck, lam_init=lam_init),
        out_shape=jax.ShapeDtypeStruct((b, s, DA_WIDTH), BF16),
        grid=(b, s // sq),
        in_specs=[full(lv), full(g), blk_q, blk_kv, blk_kv],
        out_specs=blk_q,
        scratch_shapes=[pltpu.VMEM((u, nc, sq, ck), F32), pltpu.VMEM((u, sq, LANES), F32),
                        pltpu.VMEM((u, sq, LANES), F32), pltpu.VMEM((u, sq, LANES), F32)],
        compiler_params=_params(("parallel", "arbitrary")),
        name="prompt_diff_attn",
    )(lv, g, qa, ka, va)


def _index_scores_chunk(qim, wi_t, kblk):
    total = None
    for h in range(IDX_HEADS):
        r = jnp.maximum(_dot_nt(kblk, qim[h]), 0.0)
        term = wi_t[h:h + 1, :] * r
        total = term if total is None else total + term
    return total


def _sparse_attn_kernel(qi_ref, wi_ref, ki_ref, qs_ref, ks_ref, vs_ref, o_ref,
                        key_scr, msk_scr, s_scr, m_scr, l_scr, acc_scr, *, sq, ck, k_sel, idx_bits):
    j = pl.program_id(1)
    q0 = j * sq
    nk = (q0 + sq - 1) // ck + 1
    lo, hi = _half_masks((sq, LANES))

    def masked(ref):
        out = []
        for p in range(ref.shape[1] // LANES):
            blk = ref[:, p * LANES:(p + 1) * LANES]
            out.append(jnp.where(lo, blk, jnp.zeros_like(blk)))
            out.append(jnp.where(hi, blk, jnp.zeros_like(blk)))
        return out

    qim = masked(qi_ref)
    wi_t = wi_ref[...]
    rel = (lax.broadcasted_iota(jnp.int32, (ck, sq), 1) - lax.broadcasted_iota(jnp.int32, (ck, sq), 0))

    def scores(kc, carry):
        k0 = pl.multiple_of(kc * ck, ck)
        tot = _index_scores_chunk(qim, wi_t, ki_ref[pl.ds(k0, ck), :])
        valid = (q0 - k0) + rel >= 0
        key_scr[kc] = jnp.where(valid, tot, NEG_INF)
        return carry

    lax.fori_loop(0, nk, scores, 0)
    _topk_mask(key_scr, msk_scr, nk, k_sel, idx_bits)

    qsm = masked(qs_ref)
    units = list(range(SA_HEADS))
    cols = lambda u: slice((u // 2) * LANES, (u // 2 + 1) * LANES)
    _attend(units, lambda u: qsm[u], cols, cols, _alibi_slopes(SA_HEADS), ks_ref, vs_ref, msk_scr,
            s_scr, m_scr, l_scr, acc_scr, q0, nk, sq, ck)
    for p in range(SA_HEADS // 2):
        oa = acc_scr[2 * p] / jnp.sum(l_scr[2 * p], axis=1, keepdims=True)
        ob = acc_scr[2 * p + 1] / jnp.sum(l_scr[2 * p + 1], axis=1, keepdims=True)
        o_ref[:, p * LANES:(p + 1) * LANES] = jnp.where(lo, oa, ob).astype(o_ref.dtype)


def _prompt_sparse_attn(qi, wi_t, ki2, qs, ks, vs):
    b, s, _ = qs.shape
    sq = min(s, 256)
    ck = min(s, 512)
    assert ck % sq == 0 and s % ck == 0
    nc = s // ck
    k_sel = min(TOPK_MAX, s // 4)
    idx_bits = max(1, (s - 1).bit_length())
    blk_q = lambda w: pl.BlockSpec((None, sq, w), lambda bi, j: (bi, j, 0))
    blk_kv = lambda w: pl.BlockSpec((None, s, w), lambda bi, j: (bi, 0, 0))
    u = SA_HEADS
    return pl.pallas_call(
        functools.partial(_sparse_attn_kernel, sq=sq, ck=ck, k_sel=k_sel, idx_bits=idx_bits),
        out_shape=jax.ShapeDtypeStruct((b, s, SA_WIDTH), BF16),
        grid=(b, s // sq),
        in_specs=[blk_q(IDX_HEADS * IDX_HD), pl.BlockSpec((None, IDX_HEADS, sq), lambda bi, j: (bi, 0, j)),
                  blk_kv(LANES), blk_q(SA_WIDTH), blk_kv(SA_WIDTH), blk_kv(SA_WIDTH)],
        out_specs=blk_q(SA_WIDTH),
        scratch_shapes=[pltpu.VMEM((nc, ck, sq), F32), pltpu.VMEM((nc, sq, ck), F32),
                        pltpu.VMEM((u, nc, sq, ck), F32), pltpu.VMEM((u, sq, LANES), F32),
                        pltpu.VMEM((u, sq, LANES), F32), pltpu.VMEM((u, sq, LANES), F32)],
        compiler_params=_params(("parallel", "arbitrary")),
        name="prompt_sparse_attn",
    )(qi, wi_t, ki2, qs, ks, vs)


def _sample_index_kernel(pt_ref, qi_ref, wi_ref, kin_ref, cache_ref, o_ref, buf, sem,
                         *, layer, sb, n_pages, page):
    g = pl.program_id(0)
    ng = pl.num_programs(0)
    past = n_pages * page

    def copies(step, slot):
        out = []
        for s in range(sb):
            for p in range(n_pages):
                src = cache_ref.at[layer, pt_ref[step * sb + s, p]]
                out.append(pltpu.make_async_copy(src, buf.at[slot, s, :, pl.ds(p * page, page)], sem.at[slot]))
        return out

    slot = g % 2

    @pl.when(g == 0)
    def _():
        for c in copies(0, 0):
            c.start()

    @pl.when(g + 1 < ng)
    def _():
        for c in copies(g + 1, 1 - slot):
            c.start()

    for c in copies(g, slot):
        c.wait()

    lane = lax.broadcasted_iota(jnp.int32, (1, page), 1)
    for s in range(sb):
        q = qi_ref[s]
        w = wi_ref[s]
        r = jnp.maximum(_dot(q, buf[slot, s].astype(BF16)), 0.0)
        o_ref[s:s + 1, 0:past] = jnp.sum(w * r, axis=0, keepdims=True)
        kn = kin_ref[s].astype(BF16).astype(F32)
        rn = jnp.maximum(jnp.sum(q.astype(F32) * kn, axis=1, keepdims=True), 0.0)
        tot = jnp.sum(w * rn, axis=0, keepdims=True)
        o_ref[s:s + 1, past:past + page] = jnp.where(lane == 0, tot, NEG_INF)


def _sample_index_scores(page_table, qi8, wi8, ki_new, idx_kt, layer):
    bd, n_pages = page_table.shape
    page = idx_kt.shape[3]
    sb = min(bd, SUBLANES)
    assert bd % sb == 0
    lpad = (n_pages + 1) * page
    grid_spec = pltpu.PrefetchScalarGridSpec(
        num_scalar_prefetch=1,
        grid=(bd // sb,),
        in_specs=[
            pl.BlockSpec((sb, IDX_HEADS, IDX_HD), lambda g, pt: (g, 0, 0)),
            pl.BlockSpec((sb, IDX_HEADS, 1), lambda g, pt: (g, 0, 0)),
            pl.BlockSpec((sb, 1, IDX_HD), lambda g, pt: (g, 0, 0)),
            pl.BlockSpec(memory_space=pl.ANY),
        ],
        out_specs=pl.BlockSpec((sb, lpad), lambda g, pt: (g, 0)),
        scratch_shapes=[pltpu.VMEM((2, sb, IDX_HD, n_pages * page), F32), pltpu.SemaphoreType.DMA((2,))],
    )
    return pl.pallas_call(
        functools.partial(_sample_index_kernel, layer=layer, sb=sb, n_pages=n_pages, page=page),
        out_shape=jax.ShapeDtypeStruct((bd, lpad), F32),
        grid_spec=grid_spec,
        compiler_params=_params(("arbitrary",)),
        name="sample_index_scores",
    )(page_table, qi8, wi8, ki_new, idx_kt)


def _sample_select_kernel(sc_ref, msk_ref, key_scr, msk_scr, *, k_sel, cw, idx_bits):
    nc = key_scr.shape[0]
    for c in range(nc):
        key_scr[c] = sc_ref[:, c * cw:(c + 1) * cw].T
    _topk_mask(key_scr, msk_scr, nc, k_sel, idx_bits)
    for c in range(nc):
        msk_ref[:, c * cw:(c + 1) * cw] = msk_scr[c]


def _sample_select(scores, k_sel, cw):
    bd, lpad = scores.shape
    nc = lpad // cw
    idx_bits = max(1, (lpad - 1).bit_length())
    return pl.pallas_call(
        functools.partial(_sample_select_kernel, k_sel=k_sel, cw=cw, idx_bits=idx_bits),
        out_shape=jax.ShapeDtypeStruct((bd, lpad), F32),
        grid=(1,),
        in_specs=[pl.BlockSpec((bd, lpad), lambda i: (0, 0))],
        out_specs=pl.BlockSpec((bd, lpad), lambda i: (0, 0)),
        scratch_shapes=[pltpu.VMEM((nc, cw, bd), F32), pltpu.VMEM((nc, bd, cw), F32)],
        compiler_params=_params(("arbitrary",)),
        name="sample_select",
    )(scores)


def _online_update(s, pv_fn, m_scr, l_scr, acc_scr):
    m_old = m_scr[...]
    m_new = jnp.maximum(m_old, jnp.max(s, axis=1, keepdims=True))
    m_safe = jnp.where(m_new == NEG_INF, 0.0, m_new)
    alpha = jnp.exp(m_old - m_safe)
    p = jnp.exp(s - m_safe)
    l_scr[...] = alpha * l_scr[...] + jnp.sum(p, axis=1, keepdims=True)
    acc_scr[...] = alpha * acc_scr[...] + pv_fn(p.astype(BF16))
    m_scr[...] = m_new


def _sample_attn_kernel(pt_ref, lv_ref, g_ref, qa_ref, qs_ref, kan_ref, van_ref, ksn_ref, vsn_ref,
                        msk_ref, mskn_ref, dak_hbm, dav_hbm, sak_hbm, sav_hbm, a_ref, o_ref,
                        dak_buf, dav_buf, sak_buf, sav_buf, sem, qda_scr, qsa_scr, m_scr, l_scr, acc_scr,
                        *, layer, n_pages, pg, page, lam_init):
    b = pl.program_id(0)
    hf = pl.program_id(1)
    nh = pl.num_programs(1)
    step = b * nh + hf
    n_steps = pl.num_programs(0) * nh
    slot = step % 2
    past = n_pages * page
    span = pg * page

    def copies(bb, hh, sl):
        out = []
        for j in range(pg):
            pid = pt_ref[bb, hh * pg + j]
            cols = pl.ds(j * page, page)
            out.append(pltpu.make_async_copy(dak_hbm.at[layer, pid], dak_buf.at[sl, :, cols], sem.at[sl, 0]))
            out.append(pltpu.make_async_copy(dav_hbm.at[layer, pid],
                                             dav_buf.at[sl, pl.ds(j * page * DA_HEADS, page * DA_HEADS), :],
                                             sem.at[sl, 1]))
            out.append(pltpu.make_async_copy(sak_hbm.at[layer, pid], sak_buf.at[sl, :, cols], sem.at[sl, 2]))
            out.append(pltpu.make_async_copy(sav_hbm.at[layer, pid], sav_buf.at[sl, :, cols], sem.at[sl, 3]))
        return out

    @pl.when(step == 0)
    def _():
        for c in copies(0, 0, 0):
            c.start()

    @pl.when(step + 1 < n_steps)
    def _():
        wrap = hf + 1 == nh
        for c in copies(jnp.where(wrap, b + 1, b), jnp.where(wrap, 0, hf + 1), 1 - slot):
            c.start()

    row = lax.broadcasted_iota(jnp.int32, (SUBLANES, DA_WIDTH), 0)
    col = lax.broadcasted_iota(jnp.int32, (SUBLANES, DA_WIDTH), 1)
    row1 = lax.broadcasted_iota(jnp.int32, (SUBLANES, 1), 0)
    slope_da = jnp.zeros((SUBLANES, 1), F32)
    for h, sl in enumerate(_alibi_slopes(DA_HEADS)):
        slope_da = jnp.where(row1 // 2 == h, sl, slope_da)
    slope_sa = jnp.zeros((SUBLANES, 1), F32)
    for h, sl in enumerate(_alibi_slopes(SA_HEADS)):
        slope_sa = jnp.where(row1 == h, sl, slope_sa)

    @pl.when(hf == 0)
    def _():
        qa = jnp.broadcast_to(qa_ref[...].astype(F32), (SUBLANES, DA_WIDTH))
        qda_scr[...] = jnp.where(col // DA_HD == row, qa, 0.0)
        qs = jnp.broadcast_to(qs_ref[...].astype(F32), (SUBLANES, SA_WIDTH))
        qsa_scr[...] = jnp.where(col // SA_HD == row, qs, 0.0)
        m_scr[...] = jnp.full(m_scr.shape, NEG_INF, F32)
        l_scr[...] = jnp.zeros(l_scr.shape, F32)
        acc_scr[...] = jnp.zeros(acc_scr.shape, F32)

    for c in copies(b, hf, slot):
        c.wait()

    kpos = hf * span + lax.broadcasted_iota(jnp.int32, (1, span), 1)
    dist = (past - kpos).astype(F32)

    def pv_da(p):
        return jnp.concatenate(
            [_dot(p, dav_buf[slot, pl.ds(h, span, stride=DA_HEADS), :].astype(BF16)) for h in range(DA_HEADS)],
            axis=1)

    s = _dot(qda_scr[...].astype(BF16), dak_buf[slot].astype(BF16)) - slope_da * dist
    _online_update(s, pv_da, m_scr.at[0], l_scr.at[0], acc_scr.at[0])
    s = _dot(qsa_scr[...].astype(BF16), sak_buf[slot].astype(BF16)) - slope_sa * dist + msk_ref[...]
    _online_update(s, lambda p: _dot_nt(p, sav_buf[slot].astype(BF16)), m_scr.at[1], l_scr.at[1], acc_scr.at[1])

    @pl.when(hf == nh - 1)
    def _():
        rnd = lambda ref: ref[...].astype(BF16).astype(F32)
        s = jnp.sum(qda_scr[...] * rnd(kan_ref), axis=1, keepdims=True)
        _online_update(s, lambda p: p.astype(F32) * rnd(van_ref), m_scr.at[0], l_scr.at[0], acc_scr.at[0])
        s = jnp.sum(qsa_scr[...] * rnd(ksn_ref), axis=1, keepdims=True) + mskn_ref[:, 0:1]
        _online_update(s, lambda p: p.astype(F32) * rnd(vsn_ref), m_scr.at[1], l_scr.at[1], acc_scr.at[1])

        lam = _diff_lambda(lv_ref[...], lam_init)
        oda = acc_scr[0] / l_scr[0]
        for h in range(DA_HEADS):
            cs = slice(h * LANES, (h + 1) * LANES)
            o = _diff_combine(oda[2 * h:2 * h + 1, cs], oda[2 * h + 1:2 * h + 2, cs], lam, g_ref[...], lam_init)
            a_ref[:, cs] = o.astype(a_ref.dtype)
        osa = acc_scr[1] / l_scr[1]
        lo, _ = _half_masks((1, LANES))
        for pr in range(SA_HEADS // 2):
            cs = slice(pr * LANES, (pr + 1) * LANES)
            o_ref[:, cs] = jnp.where(lo, osa[2 * pr:2 * pr + 1, cs], osa[2 * pr + 1:2 * pr + 2, cs]).astype(o_ref.dtype)


def _sample_attn(page_table, lv, g, qa, qs, ka_new, va_new, ks_new, vs_new, msk, da_kt, da_v, sa_kt, sa_vt,
                 layer, lam_init):
    bd, n_pages = page_table.shape
    page = da_kt.shape[3]
    pg = math.gcd(n_pages, 8)
    nh = n_pages // pg
    span = pg * page
    tok = lambda w: pl.BlockSpec((None, 1, w), lambda b, hf, pt: (b, 0, 0))
    full = lambda a: pl.BlockSpec(a.shape, lambda b, hf, pt: (0,) * a.ndim)
    hbm = pl.BlockSpec(memory_space=pl.ANY)
    grid_spec = pltpu.PrefetchScalarGridSpec(
        num_scalar_prefetch=1,
        grid=(bd, nh),
        in_specs=[full(lv), full(g), tok(DA_WIDTH), tok(SA_WIDTH), tok(DA_WIDTH), tok(DA_WIDTH),
                  tok(SA_WIDTH), tok(SA_WIDTH),
                  pl.BlockSpec((None, 1, span), lambda b, hf, pt: (b, 0, hf)),
                  pl.BlockSpec((None, 1, page), lambda b, hf, pt: (b, 0, n_pages)),
                  hbm, hbm, hbm, hbm],
        out_specs=[tok(DA_WIDTH), tok(SA_WIDTH)],
        scratch_shapes=[pltpu.VMEM((2, DA_WIDTH, span), F32), pltpu.VMEM((2, span * DA_HEADS, LANES), F32),
                        pltpu.VMEM((2, SA_WIDTH, span), F32), pltpu.VMEM((2, SA_WIDTH, span), F32),
                        pltpu.SemaphoreType.DMA((2, 4)),
                        pltpu.VMEM((SUBLANES, DA_WIDTH), F32), pltpu.VMEM((SUBLANES, SA_WIDTH), F32),
                        pltpu.VMEM((2, SUBLANES, 1), F32), pltpu.VMEM((2, SUBLANES, 1), F32),
                        pltpu.VMEM((2, SUBLANES, DA_WIDTH), F32)],
    )
    return pl.pallas_call(
        functools.partial(_sample_attn_kernel, layer=layer, n_pages=n_pages, pg=pg, page=page, lam_init=lam_init),
        out_shape=[jax.ShapeDtypeStruct((bd, 1, DA_WIDTH), BF16), jax.ShapeDtypeStruct((bd, 1, SA_WIDTH), BF16)],
        grid_spec=grid_spec,
        compiler_params=_params(("arbitrary", "arbitrary")),
        name="sample_attn",
    )(page_table, lv, g, qa, qs, ka_new, va_new, ks_new, vs_new, msk, msk, da_kt, da_v, sa_kt, sa_vt)


def _merge_kernel(a_ref, o_ref, ga_ref, gb_ref, x_ref, wa_ref, wb_ref, wo_ref, g_ref, b_ref, wq_ref,
                  h_ref, q_ref, *, alpha):
    merged = (jax.nn.sigmoid(ga_ref[...]) * _dot(a_ref[...], wa_ref[...])
              + jax.nn.sigmoid(gb_ref[...]) * _dot(o_ref[...], wb_ref[...]))
    h = _layer_norm(alpha * x_ref[...] + _dot(merged.astype(BF16), wo_ref[...]), g_ref[...], b_ref[...])
    h_ref[...] = h
    q = _dot(h.astype(BF16), wq_ref[...]).astype(q_ref.dtype)
    for hd in range(PEER_HEADS):
        q_ref[hd] = q[:, hd * PEER_QD:(hd + 1) * PEER_QD]


def _merge(a, o, ga, gb, x, wa, wb, wo, g, b, wq, alpha):
    t, d_model = x.shape
    tm = min(t, 256)
    row = lambda n: pl.BlockSpec((tm, n), lambda i: (i, 0))
    full = lambda arr: pl.BlockSpec(arr.shape, lambda i: (0,) * arr.ndim)
    return pl.pallas_call(
        functools.partial(_merge_kernel, alpha=alpha),
        out_shape=[jax.ShapeDtypeStruct((t, d_model), F32), jax.ShapeDtypeStruct((PEER_HEADS, t, PEER_QD), BF16)],
        grid=(t // tm,),
        in_specs=[row(DA_WIDTH), row(SA_WIDTH), row(d_model), row(d_model), row(d_model),
                  full(wa), full(wb), full(wo), full(g), full(b), full(wq)],
        out_specs=[row(d_model), pl.BlockSpec((PEER_HEADS, tm, PEER_QD), lambda i: (0, i, 0))],
        compiler_params=_params(("parallel",)),
        name="merge_ln_peerq",
    )(a, o, ga, gb, x, wa, wb, wo, g, b, wq)


def _take_max(s, ridx, exact):
    m = jnp.max(s, axis=0, keepdims=True)
    if not exact:
        return m, s == m
    first = jnp.min(jnp.where(s == m, ridx, float(s.shape[0])), axis=0, keepdims=True)
    return m, ridx == first


def _count_mismatch(flag_arrays, want):
    off = [jnp.abs(jnp.sum(f, axis=0, keepdims=True) - float(want)) for f in flag_arrays]
    return jnp.max(functools.reduce(jnp.add, off)) > 0.0


_CAND_LIM = [PEER_TOPK // (a + 1) for a in range(SUBLANES)]


def _top_subkeys(s0, s1, r0_scr, r1_scr, v_scr, exact):
    topk = PEER_TOPK
    rows, t = s0.shape
    ridx = lax.broadcasted_iota(jnp.int32, (rows, t), 0).astype(F32)
    kidx = lax.broadcasted_iota(jnp.int32, (topk, t), 0)
    r0_scr[...] = jnp.full((rows, t), float(topk), F32)
    r1_scr[...] = jnp.full((rows, t), float(topk), F32)

    def body(k, carry):
        c0, c1, v0, v1 = carry
        kf = jnp.asarray(k).astype(F32)
        m0, hit0 = _take_max(c0, ridx, exact)
        m1, hit1 = _take_max(c1, ridx, exact)
        r0_scr[...] = jnp.where(hit0, kf, r0_scr[...])
        r1_scr[...] = jnp.where(hit1, kf, r1_scr[...])
        return (jnp.where(hit0, NEG_INF, c0), jnp.where(hit1, NEG_INF, c1),
                jnp.where(kidx == k, m0, v0), jnp.where(kidx == k, m1, v1))

    zero = jnp.zeros((topk, t), F32)
    _, _, v0, v1 = lax.fori_loop(0, topk, body, (s0, s1, zero, zero))
    v_scr[0] = v0
    v_scr[1] = v1


def _top_candidates(cand, taken_scr, tv_scr, exact):
    topk = PEER_TOPK
    crows, t = cand.shape
    cidx = lax.broadcasted_iota(jnp.int32, (crows, t), 0).astype(F32)
    kidx = lax.broadcasted_iota(jnp.int32, (topk, t), 0)

    def body(k, carry):
        c, taken, tv = carry
        m, hit = _take_max(c, cidx, exact)
        return jnp.where(hit, NEG_INF, c), jnp.where(hit, 1.0, taken), jnp.where(kidx == k, m, tv)

    _, taken, tv = lax.fori_loop(0, topk, body, (cand, jnp.zeros((crows, t), F32), jnp.zeros((topk, t), F32)))
    taken_scr[...] = taken
    tv_scr[...] = tv


def _peer_route_head(q0, q1, sk0, sk1, r0_scr, r1_scr, v_scr, taken_scr, tv_scr):
    topk = PEER_TOPK
    s0 = _dot_nt(sk0, q0)
    s1 = _dot_nt(sk1, q1)
    rows, t = s0.shape

    sub = lax.broadcasted_iota(jnp.int32, (SUBLANES, t), 0)

    def select(exact):
        _top_subkeys(s0, s1, r0_scr, r1_scr, v_scr, exact)
        v0 = v_scr[0]
        v1 = v_scr[1]
        groups = [v0[0:1, :] + v1]
        for a in range(1, SUBLANES):
            groups.append(jnp.where(sub < _CAND_LIM[a], v0[a:a + 1, :] + v1[0:SUBLANES, :], NEG_INF))
        groups.append(v0[SUBLANES:topk, :] + v1[0:1, :])
        cand = jnp.concatenate(groups, axis=0)
        _top_candidates(cand, taken_scr, tv_scr, exact)

    select(exact=False)
    in_top = lambda r_scr: jnp.where(r_scr[...] < float(topk), 1.0, 0.0)

    @pl.when(_count_mismatch([in_top(r0_scr), in_top(r1_scr), taken_scr[...]], topk))
    def _():
        select(exact=True)

    v0 = v_scr[0]
    v1 = v_scr[1]
    taken = taken_scr[...]
    tv = tv_scr[...]
    z = jnp.sum(jnp.exp(tv - tv[0:1, :]), axis=0, keepdims=True)

    r0 = r0_scr[...]
    r1 = r1_scr[...]
    n_sel = jnp.zeros((rows, t), F32)
    for a in range(topk):
        if a == 0:
            n_a = jnp.sum(taken[0:topk, :], axis=0, keepdims=True)
        elif a < SUBLANES:
            base = topk + SUBLANES * (a - 1)
            n_a = jnp.sum(taken[base:base + SUBLANES, :], axis=0, keepdims=True)
        else:
            base = topk + SUBLANES * (SUBLANES - 1) + (a - SUBLANES)
            n_a = taken[base:base + 1, :]
        n_sel = jnp.where(r0 == float(a), n_a, n_sel)
    a_out = jnp.where(r0 < float(topk), jnp.exp(s0 - v0[0:1, :]) / z, 0.0)
    b_out = jnp.where(r1 < float(topk), jnp.exp(s1 - v1[0:1, :]), 0.0)
    return a_out, n_sel, b_out, r1


def _peer_route_kernel(q_ref, sk_ref, a_ref, n_ref, b_ref, r1_ref, r0_scr, r1_scr, v_scr, taken_scr, tv_scr):
    half = PEER_QD // 2

    def head(h, carry):
        a, n, b, r1 = _peer_route_head(q_ref[h, :, :half], q_ref[h, :, half:], sk_ref[h, 0], sk_ref[h, 1],
                                       r0_scr, r1_scr, v_scr, taken_scr, tv_scr)
        a_ref[h] = a
        n_ref[h] = n
        b_ref[h] = b.astype(b_ref.dtype)
        r1_ref[h] = r1.astype(r1_ref.dtype)
        return carry

    lax.fori_loop(0, PEER_HEADS, head, 0)


def _peer_route(q, sub_keys):
    t = q.shape[1]
    tt = min(t, LANES)
    blk = pl.BlockSpec((PEER_HEADS, PEER_NKEYS, tt), lambda i: (0, 0, i))
    shp = lambda dt: jax.ShapeDtypeStruct((PEER_HEADS, PEER_NKEYS, t), dt)
    return pl.pallas_call(
        _peer_route_kernel,
        out_shape=[shp(F32), shp(F32), shp(BF16), shp(BF16)],
        grid=(t // tt,),
        in_specs=[pl.BlockSpec((PEER_HEADS, tt, PEER_QD), lambda i: (0, i, 0)),
                  pl.BlockSpec(sub_keys.shape, lambda i: (0, 0, 0, 0))],
        out_specs=[blk] * 4,
        scratch_shapes=[pltpu.VMEM((PEER_NKEYS, tt), F32), pltpu.VMEM((PEER_NKEYS, tt), F32),
                        pltpu.VMEM((2, PEER_TOPK, tt), F32),
                        pltpu.VMEM((PEER_TOPK + SUBLANES * SUBLANES, tt), F32), pltpu.VMEM((PEER_TOPK, tt), F32)],
        compiler_params=_params(("parallel",)),
        name="peer_route",
    )(q, sub_keys)


def _gelu(x):
    return 0.5 * x * (1.0 + lax.erf(x * (2.0 ** -0.5)))


def _peer_expert_kernel(h_ref, u_ref, vt_ref, a_ref, n_ref, b_ref, r1_ref, g_ref, beta_ref, y_ref,
                        hb_scr, pre_scr, act_scr, acc_scr, *, ci, alpha):
    c = pl.program_id(1)

    @pl.when(c == 0)
    def _():
        hb_scr[...] = h_ref[...].astype(BF16)
        acc_scr[...] = jnp.zeros(acc_scr.shape, F32)

    pre_scr[...] = _dot_nt(u_ref[...], hb_scr[...])
    for i in range(ci):
        rows = slice(i * PEER_NKEYS, (i + 1) * PEER_NKEYS)
        w = None
        for hd in range(PEER_HEADS):
            a_row = a_ref[hd, i:i + 1, :].astype(BF16)
            n_row = n_ref[hd, i:i + 1, :].astype(BF16)
            b = b_ref[hd]
            gate = a_row * jnp.where(r1_ref[hd] < n_row, b, jnp.zeros_like(b))
            w = gate if w is None else w + gate
        act_scr[rows, :] = _gelu(pre_scr[rows, :]).astype(BF16) * w
    acc_scr[...] = acc_scr[...] + _dot(vt_ref[...], act_scr[...])

    @pl.when(c == pl.num_programs(1) - 1)
    def _():
        hv = h_ref[...]
        y_ref[...] = _layer_norm(alpha * hv + acc_scr[...].T, g_ref[...], beta_ref[...])


def _peer_experts(h, u_tab, vt_tab, ra, rn, rb, rr1, g, beta, alpha):
    t, d_model = h.shape
    tt = min(t, 512)
    ci = 2 * SUBLANES
    ne = ci * PEER_NKEYS
    nch = u_tab.shape[0] // ne
    sml = pl.BlockSpec((PEER_HEADS, ci, tt), lambda i, c: (0, c, i))
    big = pl.BlockSpec((PEER_HEADS, PEER_NKEYS, tt), lambda i, c: (0, 0, i))
    full = lambda arr: pl.BlockSpec(arr.shape, lambda i, c: (0,) * arr.ndim)
    return pl.pallas_call(
        functools.partial(_peer_expert_kernel, ci=ci, alpha=alpha),
        out_shape=jax.ShapeDtypeStruct((t, d_model), F32),
        grid=(t // tt, nch),
        in_specs=[pl.BlockSpec((tt, d_model), lambda i, c: (i, 0)),
                  pl.BlockSpec((ne, d_model), lambda i, c: (c, 0)),
                  pl.BlockSpec((d_model, ne), lambda i, c: (0, c)),
                  sml, sml, big, big, full(g), full(beta)],
        out_specs=pl.BlockSpec((tt, d_model), lambda i, c: (i, 0)),
        scratch_shapes=[pltpu.VMEM((tt, d_model), BF16), pltpu.VMEM((ne, tt), F32),
                        pltpu.VMEM((ne, tt), BF16), pltpu.VMEM((d_model, tt), F32)],
        compiler_params=_params(("parallel", "arbitrary")),
        name="peer_experts",
    )(h, u_tab, vt_tab, ra, rn, rb, rr1, g, beta)


def _layer_tail(x, a, o, ga, gb, tail, alpha):
    wa, wb, wo, ln1_g, ln1_b, wq, sub_keys, u_tab, vt_tab, ln2_g, ln2_b = tail
    h, q = _merge(a, o, ga, gb, x, wa, wb, wo, ln1_g, ln1_b, wq, alpha)
    ra, rn, rb, rr1 = _peer_route(q, sub_keys)
    return _peer_experts(h, u_tab, vt_tab, ra, rn, rb, rr1, ln2_g, ln2_b, alpha)


def kernel(x_prompt, x_sample, cache_da_k, cache_da_v, cache_sa_k, cache_sa_v, cache_idx_k, page_table, w_in, lambda_q1, lambda_k1, lambda_q2, lambda_k2, da_subln_g, w_a_up, w_b_up, w_out, ln1_g, ln1_b, peer_wq, peer_sub_keys, peer_u, peer_v, ln2_g, ln2_b):
    depth = w_in.shape[0]
    bp, sp, d_model = x_prompt.shape
    bd, sd, _ = x_sample.shape
    assert sd == 1, "the sample pass handles one new token per sequence"
    n_pool, page = cache_da_k.shape[1], cache_da_k.shape[2]
    n_pages = page_table.shape[1]
    alpha = (2.0 * depth) ** 0.25
    row2 = lambda v: v.reshape(1, -1)

    da_kt = cache_da_k.transpose(0, 1, 3, 4, 5, 2).reshape(depth, n_pool, DA_WIDTH, page)
    sa_kt = cache_sa_k.transpose(0, 1, 3, 4, 2).reshape(depth, n_pool, SA_WIDTH, page)
    sa_vt = cache_sa_v.transpose(0, 1, 3, 4, 2).reshape(depth, n_pool, SA_WIDTH, page)
    idx_kt = cache_idx_k.transpose(0, 1, 3, 2)
    da_v = cache_da_v.reshape(depth, n_pool, page * DA_HEADS, 2 * DA_HD)

    y_p = x_prompt.reshape(bp * sp, d_model)
    y_s = x_sample.reshape(bd, d_model)
    p_new, s_new = [], []
    for l in range(depth):
        lam_init = 0.8 - 0.6 * math.exp(-0.3 * l)
        lv = jnp.stack([lambda_q1[l], lambda_k1[l], lambda_q2[l], lambda_k2[l]])
        g_sub = row2(da_subln_g[l])
        w_arr = _arrange_w_in(w_in[l], d_model)
        tail = (w_a_up[l].astype(BF16), w_b_up[l].astype(BF16), w_out[l].astype(BF16), row2(ln1_g[l]),
                row2(ln1_b[l]), peer_wq[l].astype(BF16), peer_sub_keys[l].astype(BF16),
                peer_u[l].astype(BF16), peer_v[l].T.astype(BF16), row2(ln2_g[l]), row2(ln2_b[l]))

        (qa, kat, kab, vaf, vab, qs, kst, ksb, vst, vsb, qi, kit, ki2, wi, ga, gb) = _in_proj(y_p, w_arr, seq=sp)
        r3 = lambda v: v.reshape(bp, sp, v.shape[-1])
        a = _prompt_diff_attn(lv, g_sub, r3(qa), r3(kab), r3(vab), lam_init)
        wi_t = wi[:, :IDX_HEADS].reshape(bp, sp, IDX_HEADS).transpose(0, 2, 1)
        o = _prompt_sparse_attn(r3(qi), wi_t, r3(ki2), r3(qs), r3(ksb), r3(vsb))
        p_new.append((kat.reshape(bp, DA_HEADS, 2, DA_HD, sp).transpose(0, 4, 1, 2, 3),
                      vaf.reshape(bp, sp, DA_HEADS, 2 * DA_HD),
                      kst.reshape(bp, SA_HEADS, SA_HD, sp).transpose(0, 3, 1, 2),
                      vst.reshape(bp, SA_HEADS, SA_HD, sp).transpose(0, 3, 1, 2),
                      kit.transpose(0, 2, 1)))
        y_p = _layer_tail(y_p, a.reshape(bp * sp, DA_WIDTH), o.reshape(bp * sp, SA_WIDTH), ga, gb, tail, alpha)

        (qa, kaf, kab, vaf, vab, qs, ksf, ksb, vsf, vsb, qi, kif, ki2, wi, ga, gb) = _in_proj(y_s, w_arr)
        t3 = lambda v: v.reshape(bd, 1, v.shape[-1])
        scores = _sample_index_scores(page_table, qi.reshape(bd, IDX_HEADS, IDX_HD),
                                      wi[:, :IDX_HEADS].reshape(bd, IDX_HEADS, 1), t3(kif), idx_kt, l)
        k_sel = min(TOPK_MAX, (n_pages * page + 1) // 4)
        msk = _sample_select(scores, k_sel, page)
        a, o = _sample_attn(page_table, lv, g_sub, t3(qa), t3(qs), t3(kaf), t3(vaf), t3(ksf), t3(vsf),
                            msk.reshape(bd, 1, msk.shape[1]), da_kt, da_v, sa_kt, sa_vt, l, lam_init)
        s_new.append((kaf.reshape(bd, 1, DA_HEADS, 2, DA_HD), vaf.reshape(bd, 1, DA_HEADS, 2 * DA_HD),
                      ksf.reshape(bd, 1, SA_HEADS, SA_HD), vsf.reshape(bd, 1, SA_HEADS, SA_HD),
                      kif.reshape(bd, 1, IDX_HD)))
        y_s = _layer_tail(y_s, a.reshape(bd, DA_WIDTH), o.reshape(bd, SA_WIDTH), ga, gb, tail, alpha)

    stk = lambda lst, j: jnp.stack([e[j] for e in lst])
    return (y_p.reshape(bp, sp, d_model), y_s.reshape(bd, 1, d_model),
            stk(p_new, 0), stk(p_new, 1), stk(p_new, 2), stk(p_new, 3), stk(p_new, 4),
            stk(s_new, 0), stk(s_new, 1), stk(s_new, 2), stk(s_new, 3), stk(s_new, 4))
```
